```python
import math
import jax, jax.numpy as jnp
from jax import lax
import numpy as np

D_MODEL = 1024
BATCH = 8
SEQ = 2048
DEPTH = 2
DEC_BATCH = 32
DEC_SEQ = 8
PAST_LEN = 16384
PAGE_SIZE = 128

N_MIXERS = 2
N_POOL_LAYERS = (DEPTH + 1) // 2
N_ATTN_LAYERS = DEPTH // 2
POOL_WINDOWS = (2, 4, 8, 16)
POOL_GROUPS = len(POOL_WINDOWS)
POOL_CH = D_MODEL // POOL_GROUPS
POOL_BUF = max(POOL_WINDOWS) - 1
HEAD_DIM = 64
N_HEADS = D_MODEL // (2 * HEAD_DIM)
ROPE_THETA = 10000.0
Q_BLOCK = 128
N_EXPERTS = 64
TOP_K = 6
N_GROUPS = 8
TOPK_GROUPS = 4
D_EXPERT = 256
D_SHARED = 256
ROUTED_SCALE = 2.5
ALPHA = (2 * DEPTH) ** 0.25
BETA = (8 * DEPTH) ** -0.25
LN_EPS = 1e-5

kernel_name = 'hybrid_pool_diffattn_moe_step'

F32 = jnp.float32


def _layer_norm(x, g, b):
    xf = x.astype(F32)
    mu = jnp.mean(xf, axis=-1, keepdims=True)
    xc = xf - mu
    var = jnp.mean(xc * xc, axis=-1, keepdims=True)
    return (xc * lax.rsqrt(var + LN_EPS) * g + b).astype(x.dtype)


def _modulate(x, shift, scale):
    return x * (1.0 + scale[:, None, :]) + shift[:, None, :]


def _rope(x, pos):
    half = HEAD_DIM // 2
    inv_freq = ROPE_THETA ** (-jnp.arange(half, dtype=F32) * 2.0 / HEAD_DIM)
    ang = pos[:, None] * inv_freq[None, :]
    cos = jnp.cos(ang)[None, :, None, None, :].astype(x.dtype)
    sin = jnp.sin(ang)[None, :, None, None, :].astype(x.dtype)
    x1, x2 = x[..., :half], x[..., half:]
    return jnp.concatenate([x1 * cos - x2 * sin, x2 * cos + x1 * sin], axis=-1)


def _pool_mix(h_ext, start, w_pool, pool_scale):
    n, tot, _ = h_ext.shape
    L = tot - POOL_BUF
    cs = jnp.cumsum(h_ext.astype(F32), axis=1)
    cs = jnp.concatenate([jnp.zeros((n, 1, D_MODEL), F32), cs], axis=1)
    tok = h_ext[:, POOL_BUF:].astype(F32)
    pos = jnp.arange(L) + start
    parts = []
    for g, w in enumerate(POOL_WINDOWS):
        sl = slice(g * POOL_CH, (g + 1) * POOL_CH)
        win_sum = cs[:, POOL_BUF + 1:, sl] - cs[:, POOL_BUF + 1 - w:POOL_BUF + 1 - w + L, sl]
        cnt = jnp.minimum(pos + 1, w).astype(F32)[None, :, None]
        parts.append(win_sum / cnt - tok[..., sl])
    pooled = jnp.concatenate(parts, axis=-1).astype(h_ext.dtype).reshape(n, L, POOL_GROUPS, POOL_CH)
    y = jnp.einsum('nlgc,gce->nlge', pooled, w_pool).reshape(n, L, D_MODEL)
    return y * pool_scale


def _diff_qkv(h, pos, w_qkv, lq1, lk1, lq2, lk2, lam_init):
    n, L, _ = h.shape
    qkv = h @ w_qkv
    q = qkv[..., :D_MODEL].reshape(n, L, N_HEADS, 2, HEAD_DIM)
    k = qkv[..., D_MODEL:2 * D_MODEL].reshape(n, L, N_HEADS, 2, HEAD_DIM)
    v = qkv[..., 2 * D_MODEL:].reshape(n, L, N_HEADS, 2 * HEAD_DIM)
    q = _rope(q, pos) * (HEAD_DIM ** -0.5)
    k = _rope(k, pos)
    lam = (jnp.exp(jnp.sum(lq1.astype(F32) * lk1.astype(F32)))
           - jnp.exp(jnp.sum(lq2.astype(F32) * lk2.astype(F32))) + lam_init)
    return q, k, v, lam


def _prompt_attn(q, k, v):
    n, S = q.shape[:2]
    nb = S // Q_BLOCK
    qb = q.reshape(n, nb, Q_BLOCK, N_HEADS, 2, HEAD_DIM).swapaxes(0, 1)
    kpos = jnp.arange(S)
    vf = v.astype(F32)

    def block(args):
        qi, bi = args
        s = jnp.einsum('nqhjd,nkhjd->nhjqk', qi, k).astype(F32)
        qpos = bi * Q_BLOCK + jnp.arange(Q_BLOCK)
        s = jnp.where(kpos[None, :] <= qpos[:, None], s, -jnp.inf)
        p = jax.nn.softmax(s, axis=-1)
        return jnp.einsum('nhjqk,nkhv->nhjqv', p, vf)

    o = lax.map(block, (qb, jnp.arange(nb)))
    return o.transpose(1, 2, 3, 0, 4, 5).reshape(n, N_HEADS, 2, S, 2 * HEAD_DIM)


def _online_update(carry, s, v):
    m, l, acc = carry
    m_new = jnp.maximum(m, jnp.max(s, axis=-1))
    corr = jnp.exp(m - m_new)
    p = jnp.exp(s - m_new[..., None])
    l = l * corr + jnp.sum(p, axis=-1)
    acc = acc * corr[..., None] + jnp.einsum('nhjqk,nkhv->nhjqv', p, v.astype(F32))
    return (m_new, l, acc)


def _sample_attn(q, k_new, v_new, cache_k, cache_v, page_table, layer):
    n, L = q.shape[:2]
    shp = (n, N_HEADS, 2, L)
    carry = (jnp.full(shp, -jnp.inf, F32), jnp.zeros(shp, F32), jnp.zeros(shp + (2 * HEAD_DIM,), F32))

    def page_step(carry, phys):
        kp = cache_k[layer, phys]
        vp = cache_v[layer, phys]
        s = jnp.einsum('nqhjd,nkhjd->nhjqk', q, kp).astype(F32)
        return _online_update(carry, s, vp), None

    carry, _ = lax.scan(page_step, carry, page_table.T)
    s = jnp.einsum('nqhjd,nkhjd->nhjqk', q, k_new).astype(F32)
    causal = jnp.tril(jnp.ones((L, L), dtype=bool))
    s = jnp.where(causal, s, -jnp.inf)
    m, l, acc = _online_update(carry, s, v_new)
    return acc / l[..., None]


def _diff_out(o, lam, subln_g, lam_init, w_o):
    n, _, _, L, _ = o.shape
    a = o[:, :, 0] - lam * o[:, :, 1]
    a = a * lax.rsqrt(jnp.mean(a * a, axis=-1, keepdims=True) + LN_EPS) * subln_g.astype(F32)
    a = a * (1.0 - lam_init)
    a = a.transpose(0, 2, 1, 3).reshape(n, L, D_MODEL).astype(w_o.dtype)
    return a @ w_o


def _swiglu(t, wg, wu, wd):
    return (jax.nn.silu(t @ wg) * (t @ wu)) @ wd


def _moe(h, w_router, router_bias, w_gate, w_up, w_down, ws_gate, ws_up, ws_down):
    n, L, _ = h.shape
    t = h.reshape(n * L, D_MODEL)
    T = t.shape[0]
    scores = jax.nn.sigmoid((t @ w_router).astype(F32))
    choice = scores + router_bias.astype(F32)
    grp = choice.reshape(T, N_GROUPS, N_EXPERTS // N_GROUPS)
    gscore = jnp.sum(lax.top_k(grp, 2)[0], axis=-1)
    _, gidx = lax.top_k(gscore, TOPK_GROUPS)
    gmask = jnp.sum(jax.nn.one_hot(gidx, N_GROUPS, dtype=F32), axis=-2) > 0
    emask = jnp.repeat(gmask, N_EXPERTS // N_GROUPS, axis=-1)
    _, eidx = lax.top_k(jnp.where(emask, choice, -jnp.inf), TOP_K)
    wsel = jnp.take_along_axis(scores, eidx, axis=-1)
    wsel = wsel / jnp.sum(wsel, axis=-1, keepdims=True) * ROUTED_SCALE
    gates = jnp.sum(jax.nn.one_hot(eidx, N_EXPERTS, dtype=F32) * wsel[..., None], axis=-2)

    def expert_step(acc, inp):
        wg, wu, wd, g = inp
        return acc + g[:, None] * _swiglu(t, wg, wu, wd).astype(F32), None

    acc, _ = lax.scan(expert_step, jnp.zeros((T, D_MODEL), F32), (w_gate, w_up, w_down, gates.T))
    y = acc + _swiglu(t, ws_gate, ws_up, ws_down).astype(F32)
    return y.astype(h.dtype).reshape(n, L, D_MODEL)


def setup_inputs(seed: int = 0) -> dict:
    key = jax.random.key(seed)
    ks = iter(jax.random.split(key, 48))

    def nrm(shape, s):
        return jax.random.normal(next(ks), shape, F32) * s

    n_pages = PAST_LEN // PAGE_SIZE
    n_phys = (DEC_BATCH * n_pages * 5) // 4
    page_table = jax.random.permutation(next(ks), n_phys)[:DEC_BATCH * n_pages]
    page_table = page_table.reshape(DEC_BATCH, n_pages).astype(jnp.int32)
    d = D_MODEL
    w_qkv = jnp.concatenate([nrm((N_ATTN_LAYERS, d, 2 * d), d ** -0.5),
                             nrm((N_ATTN_LAYERS, d, d), d ** -0.5 * BETA)], axis=-1)
    return {
        'x_prompt': nrm((BATCH, SEQ, d), 1.0),
        'x_sample': nrm((DEC_BATCH, DEC_SEQ, d), 1.0),
        'state_pool': nrm((N_POOL_LAYERS, DEC_BATCH, POOL_BUF, d), 1.0),
        'cache_k': nrm((N_ATTN_LAYERS, n_phys, PAGE_SIZE, N_HEADS, 2, HEAD_DIM), 1.0),
        'cache_v': nrm((N_ATTN_LAYERS, n_phys, PAGE_SIZE, N_HEADS, 2 * HEAD_DIM), BETA),
        'page_table': page_table,
        'c_prompt': nrm((BATCH, d), 1.0),
        'c_sample': nrm((DEC_BATCH, d), 1.0),
        'w_ada': nrm((DEPTH, d, 6 * d), 0.5 * d ** -0.5),
        'b_ada': nrm((DEPTH, 6 * d), 0.02),
        'ln1_g': 1.0 + nrm((DEPTH, d), 0.02),
        'ln1_b': nrm((DEPTH, d), 0.02),
        'ln2_g': 1.0 + nrm((DEPTH, d), 0.02),
        'ln2_b': nrm((DEPTH, d), 0.02),
        'w_pool': nrm((N_POOL_LAYERS, POOL_GROUPS, POOL_CH, POOL_CH), POOL_CH ** -0.5 * BETA),
        'pool_scale': 1.0 + nrm((N_POOL_LAYERS, d), 0.1),
        'w_qkv': w_qkv,
        'lambda_q1': nrm((N_ATTN_LAYERS, HEAD_DIM), 0.1),
        'lambda_k1': nrm((N_ATTN_LAYERS, HEAD_DIM), 0.1),
        'lambda_q2': nrm((N_ATTN_LAYERS, HEAD_DIM), 0.1),
        'lambda_k2': nrm((N_ATTN_LAYERS, HEAD_DIM), 0.1),
        'subln_g': 1.0 + nrm((N_ATTN_LAYERS, 2 * HEAD_DIM), 0.02),
        'w_o': nrm((N_ATTN_LAYERS, d, d), d ** -0.5 * BETA),
        'w_router': nrm((DEPTH, d, N_EXPERTS), d ** -0.5),
        'router_bias': nrm((DEPTH, N_EXPERTS), 0.01),
        'w_gate': nrm((DEPTH, N_EXPERTS, d, D_EXPERT), d ** -0.5),
        'w_up': nrm((DEPTH, N_EXPERTS, d, D_EXPERT), d ** -0.5),
        'w_down': nrm((DEPTH, N_EXPERTS, D_EXPERT, d), D_EXPERT ** -0.5 * BETA),
        'ws_gate': nrm((DEPTH, d, D_SHARED), d ** -0.5),
        'ws_up': nrm((DEPTH, d, D_SHARED), d ** -0.5),
        'ws_down': nrm((DEPTH, D_SHARED, d), D_SHARED ** -0.5 * BETA),
    }


def reference(x_prompt, x_sample, state_pool, cache_k, cache_v, page_table, c_prompt, c_sample,
              w_ada, b_ada, ln1_g, ln1_b, ln2_g, ln2_b, w_pool, pool_scale, w_qkv,
              lambda_q1, lambda_k1, lambda_q2, lambda_k2, subln_g, w_o, w_router, router_bias,
              w_gate, w_up, w_down, ws_gate, ws_up, ws_down):
    xp, xs = x_prompt, x_sample
    n_p, l_p = xp.shape[:2]
    l_s = xs.shape[1]
    pos_p = jnp.arange(l_p, dtype=F32)
    pos_s = jnp.arange(l_s, dtype=F32) + PAST_LEN
    pool_p, pool_s, kp_l, vp_l, ks_l, vs_l = [], [], [], [], [], []
    for i in range(DEPTH):
        mp = jnp.split(jax.nn.silu(c_prompt) @ w_ada[i] + b_ada[i], 6, axis=-1)
        ms = jnp.split(jax.nn.silu(c_sample) @ w_ada[i] + b_ada[i], 6, axis=-1)
        hp = _modulate(xp, mp[0], mp[1])
        hs = _modulate(xs, ms[0], ms[1])
        j = i // N_MIXERS
        if i % N_MIXERS == 0:
            hp_ext = jnp.concatenate([jnp.zeros((n_p, POOL_BUF, D_MODEL), hp.dtype), hp], axis=1)
            hs_ext = jnp.concatenate([state_pool[j].astype(hs.dtype), hs], axis=1)
            out_p = _pool_mix(hp_ext, 0, w_pool[j], pool_scale[j])
            out_s = _pool_mix(hs_ext, PAST_LEN, w_pool[j], pool_scale[j])
            pool_p.append(hp_ext[:, -POOL_BUF:])
            pool_s.append(hs_ext[:, -POOL_BUF:])
        else:
            lam_init = 0.8 - 0.6 * math.exp(-0.3 * i)
            q, k, v, lam = _diff_qkv(hp, pos_p, w_qkv[j], lambda_q1[j], lambda_k1[j],
                                     lambda_q2[j], lambda_k2[j], lam_init)
            out_p = _diff_out(_prompt_attn(q, k, v), lam, subln_g[j], lam_init, w_o[j])
            kp_l.append(k)
            vp_l.append(v)
            q, k, v, lam = _diff_qkv(hs, pos_s, w_qkv[j], lambda_q1[j], lambda_k1[j],
                                     lambda_q2[j], lambda_k2[j], lam_init)
            o = _sample_attn(q, k, v, cache_k, cache_v, page_table, j)
            out_s = _diff_out(o, lam, subln_g[j], lam_init, w_o[j])
            ks_l.append(k)
            vs_l.append(v)
        xp = _layer_norm(ALPHA * xp + mp[2][:, None, :] * out_p, ln1_g[i], ln1_b[i])
        xs = _layer_norm(ALPHA * xs + ms[2][:, None, :] * out_s, ln1_g[i], ln1_b[i])
        hp = _modulate(xp, mp[3], mp[4])
        hs = _modulate(xs, ms[3], ms[4])
        fp = _moe(hp, w_router[i], router_bias[i], w_gate[i], w_up[i], w_down[i], ws_gate[i], ws_up[i], ws_down[i])
        fs = _moe(hs, w_router[i], router_bias[i], w_gate[i], w_up[i], w_down[i], ws_gate[i], ws_up[i], ws_down[i])
        xp = _layer_norm(ALPHA * xp + mp[5][:, None, :] * fp, ln2_g[i], ln2_b[i])
        xs = _layer_norm(ALPHA * xs + ms[5][:, None, :] * fs, ln2_g[i], ln2_b[i])
    pool_prompt = jnp.stack(pool_p)
    pool_sample = jnp.stack(pool_s)
    k_prompt = jnp.stack(kp_l)
    v_prompt = jnp.stack(vp_l)
    k_sample = jnp.stack(ks_l)
    v_sample = jnp.stack(vs_l)
    return (xp, xs, pool_prompt, pool_sample, k_prompt, v_prompt, k_sample, v_sample)
```

```python
import functools
import math

import jax
import jax.numpy as jnp
from jax import lax
from jax.experimental import pallas as pl
from jax.experimental.pallas import tpu as pltpu

F32 = jnp.float32
BF16 = jnp.bfloat16
HIGHEST = lax.Precision.HIGHEST

D_MODEL = 1024
DEPTH = 2
PAST_LEN = 16384
PAGE_SIZE = 128
POOL_WINDOWS = (2, 4, 8, 16)
POOL_CH = D_MODEL // len(POOL_WINDOWS)
POOL_HALO = 16
HEAD_DIM = 64
N_HEADS = D_MODEL // (2 * HEAD_DIM)
N_SUB = 2 * N_HEADS
ROPE_THETA = 10000.0
N_EXPERTS = 64
TOP_K = 6
N_GROUPS = 8
GROUP_SIZE = N_EXPERTS // N_GROUPS
TOPK_GROUPS = 4
D_EXPERT = 256
ROUTED_SCALE = 2.5
ALPHA = (2 * DEPTH) ** 0.25
LN_EPS = 1e-5
LANES = 128
VMEM_LIMIT = 56 * 1024 * 1024

NT_DIMS = (((1,), (1,)), ((), ()))


def _params(n_axes):
    return pltpu.CompilerParams(dimension_semantics=("arbitrary",) * n_axes,
                                vmem_limit_bytes=VMEM_LIMIT)


def _layer_norm(x, g, b):
    mu = jnp.mean(x, axis=-1, keepdims=True)
    xc = x - mu
    var = jnp.mean(xc * xc, axis=-1, keepdims=True)
    return xc * lax.rsqrt(var + LN_EPS) * g + b


def _mod_spec(rows, tiles_per_batch):
    return pl.BlockSpec((6, None, rows, D_MODEL), lambda i: (0, i // tiles_per_batch, 0, 0))


def _const_spec(shape):
    nd = len(shape)
    return pl.BlockSpec(shape, lambda *_: (0,) * nd)


def _ada_kernel(c_ref, w_ref, b_ref, o_ref):
    c = c_ref[...]
    s = c * jax.nn.sigmoid(c)
    o_ref[0, 0] = jnp.dot(s, w_ref[0], preferred_element_type=F32, precision=HIGHEST) + b_ref[0, 0]


def _ada(c_all, w_ada, b_ada):
    n = c_all.shape[0]
    depth = w_ada.shape[0]
    return pl.pallas_call(
        _ada_kernel,
        grid=(depth, 6),
        in_specs=[pl.BlockSpec((n, D_MODEL), lambda i, j: (0, 0)),
                  pl.BlockSpec((1, D_MODEL, D_MODEL), lambda i, j: (i, 0, j)),
                  pl.BlockSpec((1, 1, 1, D_MODEL), lambda i, j: (i, j, 0, 0))],
        out_specs=pl.BlockSpec((1, 1, n, D_MODEL), lambda i, j: (i, j, 0, 0)),
        out_shape=jax.ShapeDtypeStruct((depth, 6, n, D_MODEL), F32),
        compiler_params=_params(2),
        name="ada",
    )(c_all, w_ada, b_ada.reshape(depth, 6, 1, D_MODEL))


def _mixer_epilogue(x, y, mod_ref, g_ref, b_ref, wr_ref, x1_ref, h1_ref, lg_ref):
    x1 = _layer_norm(ALPHA * x + mod_ref[2] * y, g_ref[...], b_ref[...])
    h1 = x1 * (1.0 + mod_ref[4]) + mod_ref[3]
    x1_ref[...] = x1
    h1_ref[...] = h1.astype(BF16)
    lg_ref[...] = jnp.dot(h1, wr_ref[...], preferred_element_type=F32, precision=HIGHEST)


def _pool_kernel(x_ref, halo_ref, mod_ref, wp_ref, ps_ref, g_ref, b_ref, wr_ref,
                 x1_ref, h1_ref, lg_ref, st_ref, *, tm, tiles_per_batch, start, halo_is_state):
    t = pl.program_id(0) % tiles_per_batch
    shift, scale = mod_ref[0], mod_ref[1]
    x = x_ref[...]
    h = x * (1.0 + scale) + shift
    if halo_is_state:
        halo = halo_ref[...]
    else:
        halo = halo_ref[...] * (1.0 + scale) + shift
        halo = jnp.where(t == 0, 0.0, halo)
    ext = jnp.concatenate([halo, h], axis=0)
    c = POOL_CH
    s2 = ext + pltpu.roll(ext, 1, 0)
    s4 = s2[:, c:] + pltpu.roll(s2[:, c:], 2, 0)
    s8 = s4[:, c:] + pltpu.roll(s4[:, c:], 4, 0)
    s16 = s8[:, c:] + pltpu.roll(s8[:, c:], 8, 0)
    wins = (s2[POOL_HALO:, :c], s4[POOL_HALO:, :c], s8[POOL_HALO:, :c], s16[POOL_HALO:, :])
    pos = start + t * tm + lax.broadcasted_iota(jnp.int32, (tm, c), 0)
    ys = []
    for g, w in enumerate(POOL_WINDOWS):
        cnt = jnp.minimum(pos + 1, w).astype(F32)
        pooled = wins[g] / cnt - h[:, g * c:(g + 1) * c]
        ys.append(jnp.dot(pooled.astype(BF16), wp_ref[g], preferred_element_type=F32))
    y = jnp.concatenate(ys, axis=1) * ps_ref[...]
    st_ref[...] = ext[tm:, :]
    _mixer_epilogue(x, y, mod_ref, g_ref, b_ref, wr_ref, x1_ref, h1_ref, lg_ref)


def _pool_layer(x, halo, mods, w_pool, pool_scale, ln_g, ln_b, w_router, *, tm, tiles_per_batch,
                start, halo_is_state):
    t_tok = x.shape[0]
    n_batch = t_tok // (tm * tiles_per_batch)
    rows = mods.shape[2]
    if halo_is_state:
        halo_spec = pl.BlockSpec((None, POOL_HALO, D_MODEL), lambda i: (i, 0, 0))
    else:
        halo_spec = pl.BlockSpec((POOL_HALO, D_MODEL),
                                 lambda i: (jnp.maximum(i * (tm // POOL_HALO) - 1, 0), 0))
    kern = functools.partial(_pool_kernel, tm=tm, tiles_per_batch=tiles_per_batch, start=start,
                             halo_is_state=halo_is_state)
    return pl.pallas_call(
        kern,
        grid=(t_tok // tm,),
        in_specs=[pl.BlockSpec((tm, D_MODEL), lambda i: (i, 0)),
                  halo_spec,
                  _mod_spec(rows, tiles_per_batch),
                  _const_spec(w_pool.shape),
                  _const_spec((1, D_MODEL)), _const_spec((1, D_MODEL)), _const_spec((1, D_MODEL)),
                  _const_spec((D_MODEL, N_EXPERTS))],
        out_specs=[pl.BlockSpec((tm, D_MODEL), lambda i: (i, 0)),
                   pl.BlockSpec((tm, D_MODEL), lambda i: (i, 0)),
                   pl.BlockSpec((tm, N_EXPERTS), lambda i: (i, 0)),
                   pl.BlockSpec((None, POOL_HALO, D_MODEL), lambda i: (i // tiles_per_batch, 0, 0))],
        out_shape=[jax.ShapeDtypeStruct((t_tok, D_MODEL), F32),
                   jax.ShapeDtypeStruct((t_tok, D_MODEL), BF16),
                   jax.ShapeDtypeStruct((t_tok, N_EXPERTS), F32),
                   jax.ShapeDtypeStruct((n_batch, POOL_HALO, D_MODEL), F32)],
        compiler_params=_params(1),
        name="pool_mixer",
    )(x, halo, mods, w_pool, pool_scale, ln_g, ln_b, w_router)


def _first_max_onehot(vals, idx):
    m = vals[0]
    for v in vals[1:]:
        m = jnp.maximum(m, v)
    m = jnp.max(m, axis=0, keepdims=True)
    first = None
    for v, ix in zip(vals, idx):
        cand = jnp.min(jnp.where(v == m, ix, N_EXPERTS), axis=0, keepdims=True)
        first = cand if first is None else jnp.minimum(first, cand)
    return [ix == first for ix in idx]


def _router_kernel(lg_ref, bias_ref, gates_ref, *, tm):
    lg = lg_ref[...]
    lgt = jnp.concatenate([lg, jnp.zeros((tm, LANES - N_EXPERTS), F32)], axis=1).T
    bias = bias_ref[...]
    row = lax.broadcasted_iota(jnp.int32, (GROUP_SIZE, tm), 0)
    scores, choice, idx = [], [], []
    for g in range(N_GROUPS):
        sl = slice(g * GROUP_SIZE, (g + 1) * GROUP_SIZE)
        s = jax.nn.sigmoid(lgt[sl, :])
        scores.append(s)
        choice.append(s + bias[sl, :])
        idx.append(row + g * GROUP_SIZE)
    gscore = []
    for g in range(N_GROUPS):
        ch = choice[g]
        m1 = jnp.max(ch, axis=0, keepdims=True)
        first = jnp.min(jnp.where(ch == m1, row, GROUP_SIZE), axis=0, keepdims=True)
        m2 = jnp.max(jnp.where(row == first, -jnp.inf, ch), axis=0, keepdims=True)
        gscore.append(m1 + m2)
    masked = []
    for g in range(N_GROUPS):
        rank = jnp.zeros((1, tm), jnp.int32)
        for o in range(N_GROUPS):
            if o == g:
                continue
            beats = (gscore[o] > gscore[g]) | ((gscore[o] == gscore[g]) & (o < g))
            rank = rank + beats.astype(jnp.int32)
        masked.append(jnp.where(rank < TOPK_GROUPS, choice[g], -jnp.inf))
    sel = [jnp.zeros((GROUP_SIZE, tm), jnp.bool_) for _ in range(N_GROUPS)]
    for _ in range(TOP_K):
        hit = _first_max_onehot(masked, idx)
        sel = [s | h for s, h in zip(sel, hit)]
        masked = [jnp.where(h, -jnp.inf, v) for v, h in zip(masked, hit)]
    picked = [jnp.where(s, sc, 0.0) for s, sc in zip(sel, scores)]
    tot = picked[0]
    for p in picked[1:]:
        tot = tot + p
    tot = jnp.sum(tot, axis=0, keepdims=True)
    gates_t = jnp.concatenate([p / tot * ROUTED_SCALE for p in picked]
                              + [jnp.zeros((LANES - N_EXPERTS, tm), F32)], axis=0)
    gates_ref[...] = gates_t.T


def _router(logits, bias, *, tm):
    t_tok = logits.shape[0]
    return pl.pallas_call(
        functools.partial(_router_kernel, tm=tm),
        grid=(t_tok // tm,),
        in_specs=[pl.BlockSpec((tm, N_EXPERTS), lambda i: (i, 0)),
                  _const_spec((N_EXPERTS, 1))],
        out_specs=pl.BlockSpec((tm, LANES), lambda i: (i, 0)),
        out_shape=jax.ShapeDtypeStruct((t_tok, LANES), F32),
        compiler_params=_params(1),
        name="router",
    )(logits, bias)


def _swiglu_hidden(h, wg, wu):
    a = jnp.dot(h, wg, preferred_element_type=F32)
    u = jnp.dot(h, wu, preferred_element_type=F32)
    return a * jax.nn.sigmoid(a) * u


def _moe_kernel(h_ref, gates_ref, x_ref, mod_ref, wg_ref, wu_ref, wd_ref, sg_ref, su_ref, sd_ref,
                g_ref, b_ref, o_ref, acc_ref):
    e = pl.program_id(1)
    h = h_ref[...]

    @pl.when(e == 0)
    def _():
        hid = _swiglu_hidden(h, sg_ref[...], su_ref[...])
        acc_ref[...] = jnp.dot(hid.astype(BF16), sd_ref[...], preferred_element_type=F32)

    lane = lax.broadcasted_iota(jnp.int32, gates_ref.shape, 1)
    gate = jnp.sum(jnp.where(lane == e, gates_ref[...], 0.0), axis=1, keepdims=True)
    hid = _swiglu_hidden(h, wg_ref[0], wu_ref[0]) * gate
    acc_ref[...] += jnp.dot(hid.astype(BF16), wd_ref[0], preferred_element_type=F32)

    @pl.when(e == pl.num_programs(1) - 1)
    def _():
        o_ref[...] = _layer_norm(ALPHA * x_ref[...] + mod_ref[5] * acc_ref[...], g_ref[...], b_ref[...])


def _moe(h, gates, x, mods, wg, wu, wd, sg, su, sd, ln_g, ln_b, *, tm, tiles_per_batch):
    t_tok = h.shape[0]
    rows = mods.shape[2]
    tok = lambda i, e: (i, 0)
    return pl.pallas_call(
        _moe_kernel,
        grid=(t_tok // tm, N_EXPERTS),
        in_specs=[pl.BlockSpec((tm, D_MODEL), tok),
                  pl.BlockSpec((tm, LANES), tok),
                  pl.BlockSpec((tm, D_MODEL), tok),
                  pl.BlockSpec((6, None, rows, D_MODEL), lambda i, e: (0, i // tiles_per_batch, 0, 0)),
                  pl.BlockSpec((1, D_MODEL, D_EXPERT), lambda i, e: (e, 0, 0)),
                  pl.BlockSpec((1, D_MODEL, D_EXPERT), lambda i, e: (e, 0, 0)),
                  pl.BlockSpec((1, D_EXPERT, D_MODEL), lambda i, e: (e, 0, 0)),
                  _const_spec(sg.shape), _const_spec(su.shape), _const_spec(sd.shape),
                  _const_spec((1, D_MODEL)), _const_spec((1, D_MODEL))],
        out_specs=pl.BlockSpec((tm, D_MODEL), tok),
        out_shape=jax.ShapeDtypeStruct((t_tok, D_MODEL), F32),
        scratch_shapes=[pltpu.VMEM((tm, D_MODEL), F32)],
        compiler_params=_params(2),
        name="moe",
    )(h, gates, x, mods, wg, wu, wd, sg, su, sd, ln_g, ln_b)


def _rope_chunk(xc, cos, sin_signed, first_half):
    half = HEAD_DIM // 2
    partner = jnp.where(first_half, pltpu.roll(xc, LANES - half, 1), pltpu.roll(xc, half, 1))
    return xc * cos + partner * sin_signed


def _qkv_kernel(x_ref, mod_ref, w_ref, cos_ref, sin_ref, q_ref, k_ref, v_ref):
    h = (x_ref[...] * (1.0 + mod_ref[1]) + mod_ref[0]).astype(BF16)
    cos, sin_signed = cos_ref[...], sin_ref[...]
    lane = lax.broadcasted_iota(jnp.int32, cos.shape, 1)
    first_half = (lane % HEAD_DIM) < (HEAD_DIM // 2)
    n_chunks = D_MODEL // LANES
    for c in range(n_chunks):
        sl = slice(c * LANES, (c + 1) * LANES)
        qc = jnp.dot(h, w_ref[:, c * LANES:(c + 1) * LANES], preferred_element_type=F32)
        q_ref[:, sl] = (_rope_chunk(qc, cos, sin_signed, first_half) * (HEAD_DIM ** -0.5)).astype(BF16)
        kc = jnp.dot(h, w_ref[:, D_MODEL + c * LANES:D_MODEL + (c + 1) * LANES],
                     preferred_element_type=F32)
        k_ref[:, sl] = _rope_chunk(kc, cos, sin_signed, first_half)
    v_ref[...] = jnp.dot(h, w_ref[:, 2 * D_MODEL:], preferred_element_type=F32)


def _qkv(x, mods, w_qkv, cos, sin_signed, *, tm, tiles_per_batch, table_tiles):
    t_tok = x.shape[0]
    rows = mods.shape[2]
    tok = lambda i: (i, 0)
    return pl.pallas_call(
        _qkv_kernel,
        grid=(t_tok // tm,),
        in_specs=[pl.BlockSpec((tm, D_MODEL), tok),
                  _mod_spec(rows, tiles_per_batch),
                  _const_spec(w_qkv.shape),
                  pl.BlockSpec((tm, LANES), lambda i: (i % table_tiles, 0)),
                  pl.BlockSpec((tm, LANES), lambda i: (i % table_tiles, 0))],
        out_specs=[pl.BlockSpec((tm, D_MODEL), tok)] * 3,
        out_shape=[jax.ShapeDtypeStruct((t_tok, D_MODEL), BF16),
                   jax.ShapeDtypeStruct((t_tok, D_MODEL), F32),
                   jax.ShapeDtypeStruct((t_tok, D_MODEL), F32)],
        compiler_params=_params(1),
        name="qkv_rope",
    )(x, mods, w_qkv, cos, sin_signed)


def _lambda_value(lq1_ref, lk1_ref, lq2_ref, lk2_ref, lam_init):
    s1 = jnp.sum(lq1_ref[...] * lk1_ref[...], axis=1, keepdims=True)
    s2 = jnp.sum(lq2_ref[...] * lk2_ref[...], axis=1, keepdims=True)
    return jnp.exp(s1) - jnp.exp(s2) + lam_init


def _diff_norm(o0, o1, lam, subln_g, lam_init):
    a = o0 - lam * o1
    a = a * lax.rsqrt(jnp.mean(a * a, axis=-1, keepdims=True) + LN_EPS) * subln_g
    return a * (1.0 - lam_init)


def _online_softmax_step(s, v, m, l, acc):
    m_new = jnp.maximum(m, jnp.max(s, axis=1, keepdims=True))
    corr = jnp.exp(m - m_new)
    p = jnp.exp(s - m_new)
    l = l * corr + jnp.sum(p, axis=1, keepdims=True)
    acc = acc * corr + jnp.dot(p.astype(BF16), v, preferred_element_type=F32)
    return m_new, l, acc


def _flash_kernel(q_ref, k_ref, v_ref, lq1_ref, lk1_ref, lq2_ref, lk2_ref, sg_ref, o_ref,
                  kb_ref, vb_ref, *, tq, lam_init):
    qi = pl.program_id(2)

    @pl.when(qi == 0)
    def _():
        k = k_ref[...]
        kb_ref[0] = k[:, :HEAD_DIM].astype(BF16)
        kb_ref[1] = k[:, HEAD_DIM:].astype(BF16)
        vb_ref[...] = v_ref[...].astype(BF16)

    q_all = q_ref[...]
    rowi = lax.broadcasted_iota(jnp.int32, (tq, tq), 0)
    coli = lax.broadcasted_iota(jnp.int32, (tq, tq), 1)
    outs = []
    for j in range(2):
        q = q_all[:, j * HEAD_DIM:(j + 1) * HEAD_DIM]

        def body(kc, carry, q=q, j=j):
            off = pl.multiple_of(kc * tq, tq)
            s = lax.dot_general(q, kb_ref[j, pl.ds(off, tq), :], NT_DIMS, preferred_element_type=F32)
            return _online_softmax_step(s, vb_ref[pl.ds(off, tq), :], *carry)

        init = (jnp.full((tq, 1), -jnp.inf, F32), jnp.zeros((tq, 1), F32),
                jnp.zeros((tq, 2 * HEAD_DIM), F32))
        m, l, acc = lax.fori_loop(0, qi, body, init)
        off = pl.multiple_of(qi * tq, tq)
        s = lax.dot_general(q, kb_ref[j, pl.ds(off, tq), :], NT_DIMS, preferred_element_type=F32)
        s = jnp.where(coli <= rowi, s, -jnp.inf)
        m, l, acc = _online_softmax_step(s, vb_ref[pl.ds(off, tq), :], m, l, acc)
        outs.append(acc / l)
    lam = _lambda_value(lq1_ref, lk1_ref, lq2_ref, lk2_ref, lam_init)
    o_ref[...] = _diff_norm(outs[0], outs[1], lam, sg_ref[...], lam_init).astype(BF16)


def _flash(q, k, v, lq1, lk1, lq2, lk2, subln_g, *, n_batch, seq, tq, lam_init):
    nq = seq // tq
    lam_spec = _const_spec((1, HEAD_DIM))
    return pl.pallas_call(
        functools.partial(_flash_kernel, tq=tq, lam_init=lam_init),
        grid=(n_batch, N_HEADS, nq),
        in_specs=[pl.BlockSpec((tq, LANES), lambda b, h, i: (b * nq + i, h)),
                  pl.BlockSpec((seq, LANES), lambda b, h, i: (b, h)),
                  pl.BlockSpec((seq, LANES), lambda b, h, i: (b, h)),
                  lam_spec, lam_spec, lam_spec, lam_spec,
                  _const_spec((1, 2 * HEAD_DIM))],
        out_specs=pl.BlockSpec((tq, LANES), lambda b, h, i: (b * nq + i, h)),
        out_shape=jax.ShapeDtypeStruct((n_batch * seq, D_MODEL), BF16),
        scratch_shapes=[pltpu.VMEM((2, seq, HEAD_DIM), BF16), pltpu.VMEM((seq, 2 * HEAD_DIM), BF16)],
        compiler_params=_params(3),
        name="prompt_attn",
    )(q, k, v, lq1, lk1, lq2, lk2, subln_g)


def _paged_kernel(pt_ref, q_ref, kn_ref, vn_ref, lq1_ref, lk1_ref, lq2_ref, lk2_ref, sg_ref, *rest,
                  pages_per_step, n_new, lam_init):
    k_refs = rest[:pages_per_step]
    v_refs = rest[pages_per_step:2 * pages_per_step]
    o_ref, qbd_ref, m_ref, l_ref, acc_ref = rest[2 * pages_per_step:]
    p = pl.program_id(1)
    hw = 2 * HEAD_DIM

    @pl.when(p == 0)
    def _():
        q = q_ref[0]
        for h in range(N_HEADS):
            qh = q[:, h * hw:(h + 1) * hw]
            lane_j = lax.broadcasted_iota(jnp.int32, qh.shape, 1) // HEAD_DIM
            for j in range(2):
                qbd_ref[h, j * n_new:(j + 1) * n_new, :] = jnp.where(lane_j == j, qh, jnp.zeros_like(qh))
        m_ref[...] = jnp.full(m_ref.shape, -jnp.inf, F32)
        l_ref[...] = jnp.zeros(l_ref.shape, F32)
        acc_ref[...] = jnp.zeros(acc_ref.shape, F32)

    def update(h, s, vs):
        m_old = m_ref[h]
        m_new = jnp.maximum(m_old, jnp.max(s, axis=1, keepdims=True))
        corr = jnp.exp(m_old - m_new)
        pr = jnp.exp(s - m_new)
        l_ref[h] = l_ref[h] * corr + jnp.sum(pr, axis=1, keepdims=True)
        pv = None
        for i, vv in enumerate(vs):
            part = jnp.dot(pr[:, i * PAGE_SIZE:(i + 1) * PAGE_SIZE].astype(BF16), vv,
                           preferred_element_type=F32)
            pv = part if pv is None else pv + part
        acc_ref[h] = acc_ref[h] * corr + pv
        m_ref[h] = m_new

    for h in range(N_HEADS):
        qh = qbd_ref[h]
        s = jnp.concatenate(
            [jnp.dot(qh, kr[0, 0, h * hw:(h + 1) * hw, :].astype(BF16), preferred_element_type=F32)
             for kr in k_refs], axis=1)
        update(h, s, [vr[0, 0, :, h, :].astype(BF16) for vr in v_refs])

    @pl.when(p == pl.num_programs(1) - 1)
    def _():
        lam = _lambda_value(lq1_ref, lk1_ref, lq2_ref, lk2_ref, lam_init)
        for h in range(N_HEADS):
            cols = slice(h * hw, (h + 1) * hw)
            s_new = lax.dot_general(qbd_ref[h], kn_ref[0, :, cols].astype(BF16), NT_DIMS,
                                    preferred_element_type=F32)
            tok = lax.broadcasted_iota(jnp.int32, s_new.shape, 0) % n_new
            key = lax.broadcasted_iota(jnp.int32, s_new.shape, 1)
            s_new = jnp.where((key < n_new) & (key <= tok), s_new, -jnp.inf)
            update(h, s_new, [vn_ref[0, :, cols].astype(BF16)])
            o = acc_ref[h] / l_ref[h]
            o_ref[0, :, cols] = _diff_norm(o[:n_new], o[n_new:], lam, sg_ref[...], lam_init).astype(BF16)


def _paged(page_table, q, k_new, v_new, cache_kt, cache_v, lq1, lk1, lq2, lk2, subln_g, *,
           layer, pages_per_step, lam_init):
    n_batch, n_pages = page_table.shape
    n_new = q.shape[1]
    steps = n_pages // pages_per_step
    lam_spec = pl.BlockSpec((1, HEAD_DIM), lambda n, p, pt: (0, 0))

    def kpage_spec(i):
        return pl.BlockSpec((1, 1, D_MODEL, PAGE_SIZE),
                            lambda n, p, pt: (layer, pt[n * n_pages + p * pages_per_step + i], 0, 0))

    def vpage_spec(i):
        return pl.BlockSpec((1, 1, PAGE_SIZE, N_HEADS, 2 * HEAD_DIM),
                            lambda n, p, pt: (layer, pt[n * n_pages + p * pages_per_step + i], 0, 0, 0))

    per_batch = lambda n, p, pt: (n, 0, 0)
    grid_spec = pltpu.PrefetchScalarGridSpec(
        num_scalar_prefetch=1,
        grid=(n_batch, steps),
        in_specs=[pl.BlockSpec((1, n_new, D_MODEL), per_batch),
                  pl.BlockSpec((1, PAGE_SIZE, D_MODEL), per_batch),
                  pl.BlockSpec((1, PAGE_SIZE, D_MODEL), per_batch),
                  lam_spec, lam_spec, lam_spec, lam_spec,
                  pl.BlockSpec((1, 2 * HEAD_DIM), lambda n, p, pt: (0, 0))]
                 + [kpage_spec(i) for i in range(pages_per_step)]
                 + [vpage_spec(i) for i in range(pages_per_step)],
        out_specs=pl.BlockSpec((1, n_new, D_MODEL), per_batch),
        scratch_shapes=[pltpu.VMEM((N_HEADS, 2 * n_new, 2 * HEAD_DIM), BF16),
                        pltpu.VMEM((N_HEADS, 2 * n_new, 1), F32),
                        pltpu.VMEM((N_HEADS, 2 * n_new, 1), F32),
                        pltpu.VMEM((N_HEADS, 2 * n_new, 2 * HEAD_DIM), F32)],
    )
    return pl.pallas_call(
        functools.partial(_paged_kernel, pages_per_step=pages_per_step, n_new=n_new, lam_init=lam_init),
        grid_spec=grid_spec,
        out_shape=jax.ShapeDtypeStruct((n_batch, n_new, D_MODEL), BF16),
        compiler_params=_params(2),
        name="sample_attn",
    )(page_table.reshape(-1), q, k_new, v_new, lq1, lk1, lq2, lk2, subln_g,
      *([cache_kt] * pages_per_step), *([cache_v] * pages_per_step))


def _attn_out_kernel(a_ref, x_ref, mod_ref, wo_ref, g_ref, b_ref, wr_ref, x1_ref, h1_ref, lg_ref):
    y = jnp.dot(a_ref[...], wo_ref[...], preferred_element_type=F32)
    _mixer_epilogue(x_ref[...], y, mod_ref, g_ref, b_ref, wr_ref, x1_ref, h1_ref, lg_ref)


def _attn_out(a, x, mods, w_o, ln_g, ln_b, w_router, *, tm, tiles_per_batch):
    t_tok = x.shape[0]
    rows = mods.shape[2]
    tok = lambda i: (i, 0)
    return pl.pallas_call(
        _attn_out_kernel,
        grid=(t_tok // tm,),
        in_specs=[pl.BlockSpec((tm, D_MODEL), tok), pl.BlockSpec((tm, D_MODEL), tok),
                  _mod_spec(rows, tiles_per_batch), _const_spec(w_o.shape),
                  _const_spec((1, D_MODEL)), _const_spec((1, D_MODEL)),
                  _const_spec((D_MODEL, N_EXPERTS))],
        out_specs=[pl.BlockSpec((tm, D_MODEL), tok), pl.BlockSpec((tm, D_MODEL), tok),
                   pl.BlockSpec((tm, N_EXPERTS), tok)],
        out_shape=[jax.ShapeDtypeStruct((t_tok, D_MODEL), F32),
                   jax.ShapeDtypeStruct((t_tok, D_MODEL), BF16),
                   jax.ShapeDtypeStruct((t_tok, N_EXPERTS), F32)],
        compiler_params=_params(1),
        name="attn_out",
    )(a, x, mods, w_o, ln_g, ln_b, w_router)


def _rope_tables(pos):
    half = HEAD_DIM // 2
    inv_freq = ROPE_THETA ** (-jnp.arange(half, dtype=F32) * 2.0 / HEAD_DIM)
    ang = pos[:, None] * inv_freq[None, :]
    cos, sin = jnp.cos(ang), jnp.sin(ang)
    reps = LANES // HEAD_DIM
    cos_t = jnp.tile(jnp.concatenate([cos, cos], axis=1), (1, reps))
    sin_t = jnp.tile(jnp.concatenate([-sin, sin], axis=1), (1, reps))
    return cos_t, sin_t


def kernel(x_prompt, x_sample, state_pool, cache_k, cache_v, page_table, c_prompt, c_sample, w_ada, b_ada, ln1_g, ln1_b, ln2_g, ln2_b, w_pool, pool_scale, w_qkv, lambda_q1, lambda_k1, lambda_q2, lambda_k2, subln_g, w_o, w_router, router_bias, w_gate, w_up, w_down, ws_gate, ws_up, ws_down):
    n_p, l_p, d = x_prompt.shape
    n_s, l_s, _ = x_sample.shape
    t_p, t_s = n_p * l_p, n_s * l_s
    tm_p = 512
    tpb_p = l_p // tm_p
    tm_moe = 1024
    row = lambda a: a.reshape(1, -1)

    mods = _ada(jnp.concatenate([c_prompt, c_sample], axis=0), w_ada, b_ada)
    xp = x_prompt.reshape(t_p, d)
    xs = x_sample.reshape(t_s, d)
    cache_kt = jnp.transpose(cache_k, (0, 1, 3, 4, 5, 2)).reshape(cache_k.shape[0], cache_k.shape[1], d,
                                                                   PAGE_SIZE)
    cos_p, sin_p = _rope_tables(jnp.arange(l_p, dtype=F32))
    cos_s, sin_s = _rope_tables(jnp.arange(l_s, dtype=F32) + PAST_LEN)
    cos_s, sin_s = jnp.tile(cos_s, (n_s, 1)), jnp.tile(sin_s, (n_s, 1))

    pool_p, pool_s, kp_l, vp_l, ks_l, vs_l = [], [], [], [], [], []
    for i in range(DEPTH):
        j = i // 2
        mods_p = mods[i, :, :n_p].reshape(6, n_p, 1, d)
        mods_s_batch = mods[i, :, n_p:].reshape(6, n_s, 1, d)
        mods_s_tok = jnp.repeat(mods[i, :, n_p:], l_s, axis=1).reshape(6, 1, t_s, d)
        g1, b1, g2, b2 = row(ln1_g[i]), row(ln1_b[i]), row(ln2_g[i]), row(ln2_b[i])
        if i % 2 == 0:
            wp = w_pool[j].astype(BF16)
            ps = row(pool_scale[j])
            xp1, hp1, lgp, st_p = _pool_layer(xp, xp, mods_p, wp, ps, g1, b1, w_router[i], tm=tm_p,
                                              tiles_per_batch=tpb_p, start=0, halo_is_state=False)
            hist = jnp.pad(state_pool[j], ((0, 0), (POOL_HALO - state_pool.shape[2], 0), (0, 0)))
            xs1, hs1, lgs, st_s = _pool_layer(xs, hist, mods_s_batch, wp, ps, g1, b1, w_router[i], tm=l_s,
                                              tiles_per_batch=1, start=PAST_LEN, halo_is_state=True)
            keep = state_pool.shape[2]
            pool_p.append(st_p[:, POOL_HALO - keep:])
            pool_s.append(st_s[:, POOL_HALO - keep:])
        else:
            lam_init = 0.8 - 0.6 * math.exp(-0.3 * i)
            wq = w_qkv[j].astype(BF16)
            wo = w_o[j].astype(BF16)
            lams = (row(lambda_q1[j]), row(lambda_k1[j]), row(lambda_q2[j]), row(lambda_k2[j]))
            sg = row(subln_g[j])
            qp, kp, vp = _qkv(xp, mods_p, wq, cos_p, sin_p, tm=tm_p, tiles_per_batch=tpb_p,
                              table_tiles=tpb_p)
            ap = _flash(qp, kp, vp, *lams, sg, n_batch=n_p, seq=l_p, tq=512, lam_init=lam_init)
            xp1, hp1, lgp = _attn_out(ap, xp, mods_p, wo, g1, b1, w_router[i], tm=tm_p,
                                      tiles_per_batch=tpb_p)
            qs, ks, vs = _qkv(xs, mods_s_tok, wq, cos_s, sin_s, tm=t_s, tiles_per_batch=1, table_tiles=1)
            pad_new = lambda a: jnp.pad(a.reshape(n_s, l_s, d), ((0, 0), (0, PAGE_SIZE - l_s), (0, 0)))
            a_s = _paged(page_table, qs.reshape(n_s, l_s, d), pad_new(ks), pad_new(vs), cache_kt, cache_v,
                         *lams, sg, layer=j, pages_per_step=4, lam_init=lam_init)
            xs1, hs1, lgs = _attn_out(a_s.reshape(t_s, d), xs, mods_s_tok, wo, g1, b1, w_router[i],
                                      tm=t_s, tiles_per_batch=1)
            kp_l.append(kp.reshape(n_p, l_p, N_HEADS, 2, HEAD_DIM))
            vp_l.append(vp.reshape(n_p, l_p, N_HEADS, 2 * HEAD_DIM))
            ks_l.append(ks.reshape(n_s, l_s, N_HEADS, 2, HEAD_DIM))
            vs_l.append(vs.reshape(n_s, l_s, N_HEADS, 2 * HEAD_DIM))
        bias = router_bias[i].reshape(N_EXPERTS, 1)
        wg, wu, wd = w_gate[i].astype(BF16), w_up[i].astype(BF16), w_down[i].astype(BF16)
        sgate, sup, sdown = ws_gate[i].astype(BF16), ws_up[i].astype(BF16), ws_down[i].astype(BF16)
        gates_p = _router(lgp, bias, tm=256)
        gates_s = _router(lgs, bias, tm=256)
        xp = _moe(hp1, gates_p, xp1, mods_p, wg, wu, wd, sgate, sup, sdown, g2, b2, tm=tm_moe,
                  tiles_per_batch=l_p // tm_moe)
        xs = _moe(hs1, gates_s, xs1, mods_s_tok, wg, wu, wd, sgate, sup, sdown, g2, b2, tm=t_s,
                  tiles_per_batch=1)
    return (xp.reshape(n_p, l_p, d), xs.reshape(n_s, l_s, d), jnp.stack(pool_p), jnp.stack(pool_s),
            jnp.stack(kp_l), jnp.stack(vp_l), jnp.stack(ks_l), jnp.stack(vs_l))
```

```python
import functools
import math

import jax
import jax.numpy as jnp
from jax import lax
from jax.experimental import pallas as pl
from jax.experimental.pallas import tpu as pltpu

F32 = jnp.float32
BF16 = jnp.bfloat16
HIGHEST = lax.Precision.HIGHEST

D_MODEL = 1024
DEPTH = 2
PAST_LEN = 16384
PAGE_SIZE = 128
POOL_WINDOWS = (2, 4, 8, 16)
POOL_CH = D_MODEL // len(POOL_WINDOWS)
POOL_HALO = 16
HEAD_DIM = 64
N_HEADS = D_MODEL // (2 * HEAD_DIM)
N_SUB = 2 * N_HEADS
ROPE_THETA = 10000.0
N_EXPERTS = 64
TOP_K = 6
N_GROUPS = 8
GROUP_SIZE = N_EXPERTS // N_GROUPS
TOPK_GROUPS = 4
D_EXPERT = 256
ROUTED_SCALE = 2.5
ALPHA = (2 * DEPTH) ** 0.25
LN_EPS = 1e-5
LANES = 128
VMEM_LIMIT = 56 * 1024 * 1024

NT_DIMS = (((1,), (1,)), ((), ()))


def _params(n_axes):
    return pltpu.CompilerParams(dimension_semantics=("arbitrary",) * n_axes,
                                vmem_limit_bytes=VMEM_LIMIT)


def _layer_norm(x, g, b):
    mu = jnp.mean(x, axis=-1, keepdims=True)
    xc = x - mu
    var = jnp.mean(xc * xc, axis=-1, keepdims=True)
    return xc * lax.rsqrt(var + LN_EPS) * g + b


def _mod_spec(rows, tiles_per_batch):
    return pl.BlockSpec((6, None, rows, D_MODEL), lambda i: (0, i // tiles_per_batch, 0, 0))


def _const_spec(shape):
    nd = len(shape)
    return pl.BlockSpec(shape, lambda *_: (0,) * nd)


def _ada_kernel(c_ref, w_ref, b_ref, o_ref):
    c = c_ref[...]
    s = c * jax.nn.sigmoid(c)
    o_ref[0, 0] = jnp.dot(s, w_ref[0], preferred_element_type=F32, precision=HIGHEST) + b_ref[0, 0]


def _ada(c_all, w_ada, b_ada):
    n = c_all.shape[0]
    depth = w_ada.shape[0]
    return pl.pallas_call(
        _ada_kernel,
        grid=(depth, 6),
        in_specs=[pl.BlockSpec((n, D_MODEL), lambda i, j: (0, 0)),
                  pl.BlockSpec((1, D_MODEL, D_MODEL), lambda i, j: (i, 0, j)),
                  pl.BlockSpec((1, 1, 1, D_MODEL), lambda i, j: (i, j, 0, 0))],
        out_specs=pl.BlockSpec((1, 1, n, D_MODEL), lambda i, j: (i, j, 0, 0)),
        out_shape=jax.ShapeDtypeStruct((depth, 6, n, D_MODEL), F32),
        compiler_params=_params(2),
        name="ada",
    )(c_all, w_ada, b_ada.reshape(depth, 6, 1, D_MODEL))


def _mixer_epilogue(x, y, mod_ref, g_ref, b_ref, wr_ref, x1_ref, h1_ref, lg_ref):
    x1 = _layer_norm(ALPHA * x + mod_ref[2] * y, g_ref[...], b_ref[...])
    h1 = x1 * (1.0 + mod_ref[4]) + mod_ref[3]
    x1_ref[...] = x1
    h1_ref[...] = h1.astype(BF16)
    lg_ref[...] = jnp.dot(h1, wr_ref[...], preferred_element_type=F32, precision=HIGHEST)


def _pool_kernel(x_ref, halo_ref, mod_ref, wp_ref, ps_ref, g_ref, b_ref, wr_ref,
                 x1_ref, h1_ref, lg_ref, st_ref, *, tm, tiles_per_batch, start, halo_is_state):
    t = pl.program_id(0) % tiles_per_batch
    shift, scale = mod_ref[0], mod_ref[1]
    x = x_ref[...]
    h = x * (1.0 + scale) + shift
    if halo_is_state:
        halo = halo_ref[...]
    else:
        halo = halo_ref[...] * (1.0 + scale) + shift
        halo = jnp.where(t == 0, 0.0, halo)
    ext = jnp.concatenate([halo, h], axis=0)
    c = POOL_CH
    s2 = ext + pltpu.roll(ext, 1, 0)
    s4 = s2[:, c:] + pltpu.roll(s2[:, c:], 2, 0)
    s8 = s4[:, c:] + pltpu.roll(s4[:, c:], 4, 0)
    s16 = s8[:, c:] + pltpu.roll(s8[:, c:], 8, 0)
    wins = (s2[POOL_HALO:, :c], s4[POOL_HALO:, :c], s8[POOL_HALO:, :c], s16[POOL_HALO:, :])
    pos = start + t * tm + lax.broadcasted_iota(jnp.int32, (tm, c), 0)
    ys = []
    for g, w in enumerate(POOL_WINDOWS):
        cnt = jnp.minimum(pos + 1, w).astype(F32)
        pooled = wins[g] / cnt - h[:, g * c:(g + 1) * c]
        ys.append(jnp.dot(pooled.astype(BF16), wp_ref[g], preferred_element_type=F32))
    y = jnp.concatenate(ys, axis=1) * ps_ref[...]
    st_ref[...] = ext[tm:, :]
    _mixer_epilogue(x, y, mod_ref, g_ref, b_ref, wr_ref, x1_ref, h1_ref, lg_ref)


def _pool_layer(x, halo, mods, w_pool, pool_scale, ln_g, ln_b, w_router, *, tm, tiles_per_batch,
                start, halo_is_state):
    t_tok = x.shape[0]
    n_batch = t_tok // (tm * tiles_per_batch)
    rows = mods.shape[2]
    if halo_is_state:
        halo_spec = pl.BlockSpec((None, POOL_HALO, D_MODEL), lambda i: (i, 0, 0))
    else:
        halo_spec = pl.BlockSpec((POOL_HALO, D_MODEL),
                                 lambda i: (jnp.maximum(i * (tm // POOL_HALO) - 1, 0), 0))
    kern = functools.partial(_pool_kernel, tm=tm, tiles_per_batch=tiles_per_batch, start=start,
                             halo_is_state=halo_is_state)
    return pl.pallas_call(
        kern,
        grid=(t_tok // tm,),
        in_specs=[pl.BlockSpec((tm, D_MODEL), lambda i: (i, 0)),
                  halo_spec,
                  _mod_spec(rows, tiles_per_batch),
                  _const_spec(w_pool.shape),
                  _const_spec((1, D_MODEL)), _const_spec((1, D_MODEL)), _const_spec((1, D_MODEL)),
                  _const_spec((D_MODEL, N_EXPERTS))],
        out_specs=[pl.BlockSpec((tm, D_MODEL), lambda i: (i, 0)),
                   pl.BlockSpec((tm, D_MODEL), lambda i: (i, 0)),
                   pl.BlockSpec((tm, N_EXPERTS), lambda i: (i, 0)),
                   pl.BlockSpec((None, POOL_HALO, D_MODEL), lambda i: (i // tiles_per_batch, 0, 0))],
        out_shape=[jax.ShapeDtypeStruct((t_tok, D_MODEL), F32),
                   jax.ShapeDtypeStruct((t_tok, D_MODEL), BF16),
                   jax.ShapeDtypeStruct((t_tok, N_EXPERTS), F32),
                   jax.ShapeDtypeStruct((n_batch, POOL_HALO, D_MODEL), F32)],
        compiler_params=_params(1),
        name="pool_mixer",
    )(x, halo, mods, w_pool, pool_scale, ln_g, ln_b, w_router)


def _first_max_onehot(vals, idx):
    m = vals[0]
    for v in vals[1:]:
        m = jnp.maximum(m, v)
    m = jnp.max(m, axis=0, keepdims=True)
    first = None
    for v, ix in zip(vals, idx):
        cand = jnp.min(jnp.where(v == m, ix, N_EXPERTS), axis=0, keepdims=True)
        first = cand if first is None else jnp.minimum(first, cand)
    return [ix == first for ix in idx]


def _router_kernel(lg_ref, bias_ref, gates_ref, *, tm):
    lg = lg_ref[...]
    lgt = jnp.concatenate([lg, jnp.zeros((tm, LANES - N_EXPERTS), F32)], axis=1).T
    bias = bias_ref[...]
    row = lax.broadcasted_iota(jnp.int32, (GROUP_SIZE, tm), 0)
    scores, choice, idx = [], [], []
    for g in range(N_GROUPS):
        sl = slice(g * GROUP_SIZE, (g + 1) * GROUP_SIZE)
        s = jax.nn.sigmoid(lgt[sl, :])
        scores.append(s)
        choice.append(s + bias[sl, :])
        idx.append(row + g * GROUP_SIZE)
    gscore = []
    for g in range(N_GROUPS):
        ch = choice[g]
        m1 = jnp.max(ch, axis=0, keepdims=True)
        first = jnp.min(jnp.where(ch == m1, row, GROUP_SIZE), axis=0, keepdims=True)
        m2 = jnp.max(jnp.where(row == first, -jnp.inf, ch), axis=0, keepdims=True)
        gscore.append(m1 + m2)
    masked = []
    for g in range(N_GROUPS):
        rank = jnp.zeros((1, tm), jnp.int32)
        for o in range(N_GROUPS):
            if o == g:
                continue
            beats = (gscore[o] > gscore[g]) | ((gscore[o] == gscore[g]) & (o < g))
            rank = rank + beats.astype(jnp.int32)
        masked.append(jnp.where(rank < TOPK_GROUPS, choice[g], -jnp.inf))
    sel = [jnp.zeros((GROUP_SIZE, tm), jnp.bool_) for _ in range(N_GROUPS)]
    for _ in range(TOP_K):
        hit = _first_max_onehot(masked, idx)
        sel = [s | h for s, h in zip(sel, hit)]
        masked = [jnp.where(h, -jnp.inf, v) for v, h in zip(masked, hit)]
    picked = [jnp.where(s, sc, 0.0) for s, sc in zip(sel, scores)]
    tot = picked[0]
    for p in picked[1:]:
        tot = tot + p
    tot = jnp.sum(tot, axis=0, keepdims=True)
    gates_t = jnp.concatenate([p / tot * ROUTED_SCALE for p in picked]
                              + [jnp.zeros((LANES - N_EXPERTS, tm), F32)], axis=0)
    gates_ref[...] = gates_t.T


def _router(logits, bias, *, tm):
    t_tok = logits.shape[0]
    return pl.pallas_call(
        functools.partial(_router_kernel, tm=tm),
        grid=(t_tok // tm,),
        in_specs=[pl.BlockSpec((tm, N_EXPERTS), lambda i: (i, 0)),
                  _const_spec((N_EXPERTS, 1))],
        out_specs=pl.BlockSpec((tm, LANES), lambda i: (i, 0)),
        out_shape=jax.ShapeDtypeStruct((t_tok, LANES), F32),
        compiler_params=_params(1),
        name="router",
    )(logits, bias)


def _swiglu_hidden(h, wg, wu):
    a = jnp.dot(h, wg, preferred_element_type=F32)
    u = jnp.dot(h, wu, preferred_element_type=F32)
    return a * jax.nn.sigmoid(a) * u


def _moe_kernel(h_ref, gates_ref, x_ref, mod_ref, wg_ref, wu_ref, wd_ref, sg_ref, su_ref, sd_ref,
                g_ref, b_ref, o_ref, acc_ref):
    e = pl.program_id(1)
    h = h_ref[...]

    @pl.when(e == 0)
    def _():
        hid = _swiglu_hidden(h, sg_ref[0].astype(BF16), su_ref[0].astype(BF16))
        acc_ref[...] = jnp.dot(hid.astype(BF16), sd_ref[0].astype(BF16), preferred_element_type=F32)

    lane = lax.broadcasted_iota(jnp.int32, gates_ref.shape, 1)
    gate = jnp.sum(jnp.where(lane == e, gates_ref[...], 0.0), axis=1, keepdims=True)
    hid = _swiglu_hidden(h, wg_ref[0, 0].astype(BF16), wu_ref[0, 0].astype(BF16)) * gate
    acc_ref[...] += jnp.dot(hid.astype(BF16), wd_ref[0, 0].astype(BF16), preferred_element_type=F32)

    @pl.when(e == pl.num_programs(1) - 1)
    def _():
        o_ref[...] = _layer_norm(ALPHA * x_ref[...] + mod_ref[5] * acc_ref[...], g_ref[...], b_ref[...])


def _moe(h, gates, x, mods, wg, wu, wd, sg, su, sd, ln_g, ln_b, *, layer, tm, tiles_per_batch):
    t_tok = h.shape[0]
    rows = mods.shape[2]
    tok = lambda i, e: (i, 0)
    expert = lambda i, e: (layer, e, 0, 0)
    shared = lambda i, e: (layer, 0, 0)
    return pl.pallas_call(
        _moe_kernel,
        grid=(t_tok // tm, N_EXPERTS),
        in_specs=[pl.BlockSpec((tm, D_MODEL), tok),
                  pl.BlockSpec((tm, LANES), tok),
                  pl.BlockSpec((tm, D_MODEL), tok),
                  pl.BlockSpec((6, None, rows, D_MODEL), lambda i, e: (0, i // tiles_per_batch, 0, 0)),
                  pl.BlockSpec((1, 1, D_MODEL, D_EXPERT), expert),
                  pl.BlockSpec((1, 1, D_MODEL, D_EXPERT), expert),
                  pl.BlockSpec((1, 1, D_EXPERT, D_MODEL), expert),
                  pl.BlockSpec((1,) + sg.shape[1:], shared), pl.BlockSpec((1,) + su.shape[1:], shared),
                  pl.BlockSpec((1,) + sd.shape[1:], shared),
                  _const_spec((1, D_MODEL)), _const_spec((1, D_MODEL))],
        out_specs=pl.BlockSpec((tm, D_MODEL), tok),
        out_shape=jax.ShapeDtypeStruct((t_tok, D_MODEL), F32),
        scratch_shapes=[pltpu.VMEM((tm, D_MODEL), F32)],
        compiler_params=_params(2),
        name="moe",
    )(h, gates, x, mods, wg, wu, wd, sg, su, sd, ln_g, ln_b)


def _rope_chunk(xc, cos, sin_signed, first_half):
    half = HEAD_DIM // 2
    partner = jnp.where(first_half, pltpu.roll(xc, LANES - half, 1), pltpu.roll(xc, half, 1))
    return xc * cos + partner * sin_signed


def _qkv_kernel(x_ref, mod_ref, w_ref, cos_ref, sin_ref, q_ref, k_ref, v_ref):
    h = (x_ref[...] * (1.0 + mod_ref[1]) + mod_ref[0]).astype(BF16)
    cos, sin_signed = cos_ref[...], sin_ref[...]
    lane = lax.broadcasted_iota(jnp.int32, cos.shape, 1)
    first_half = (lane % HEAD_DIM) < (HEAD_DIM // 2)
    n_chunks = D_MODEL // LANES
    for c in range(n_chunks):
        sl = slice(c * LANES, (c + 1) * LANES)
        qc = jnp.dot(h, w_ref[:, c * LANES:(c + 1) * LANES], preferred_element_type=F32)
        q_ref[:, sl] = (_rope_chunk(qc, cos, sin_signed, first_half) * (HEAD_DIM ** -0.5)).astype(BF16)
        kc = jnp.dot(h, w_ref[:, D_MODEL + c * LANES:D_MODEL + (c + 1) * LANES],
                     preferred_element_type=F32)
        k_ref[:, sl] = _rope_chunk(kc, cos, sin_signed, first_half)
    v_ref[...] = jnp.dot(h, w_ref[:, 2 * D_MODEL:], preferred_element_type=F32)


def _qkv(x, mods, w_qkv, cos, sin_signed, *, tm, tiles_per_batch, table_tiles):
    t_tok = x.shape[0]
    rows = mods.shape[2]
    tok = lambda i: (i, 0)
    return pl.pallas_call(
        _qkv_kernel,
        grid=(t_tok // tm,),
        in_specs=[pl.BlockSpec((tm, D_MODEL), tok),
                  _mod_spec(rows, tiles_per_batch),
                  _const_spec(w_qkv.shape),
                  pl.BlockSpec((tm, LANES), lambda i: (i % table_tiles, 0)),
                  pl.BlockSpec((tm, LANES), lambda i: (i % table_tiles, 0))],
        out_specs=[pl.BlockSpec((tm, D_MODEL), tok)] * 3,
        out_shape=[jax.ShapeDtypeStruct((t_tok, D_MODEL), BF16),
                   jax.ShapeDtypeStruct((t_tok, D_MODEL), F32),
                   jax.ShapeDtypeStruct((t_tok, D_MODEL), F32)],
        compiler_params=_params(1),
        name="qkv_rope",
    )(x, mods, w_qkv, cos, sin_signed)


def _qkv_prompt_kernel(x_ref, mod_ref, wq_ref, wkt_ref, wv_ref, cos_ref, sin_ref, cost_ref, sint_ref,
                       q_ref, kt_ref, v_ref, *, tm):
    h = (x_ref[...] * (1.0 + mod_ref[1]) + mod_ref[0]).astype(BF16)
    cos, sin_signed = cos_ref[...], sin_ref[...]
    lane = lax.broadcasted_iota(jnp.int32, cos.shape, 1)
    first_half = (lane % HEAD_DIM) < (HEAD_DIM // 2)
    for c in range(D_MODEL // LANES):
        sl = slice(c * LANES, (c + 1) * LANES)
        qc = jnp.dot(h, wq_ref[:, sl], preferred_element_type=F32)
        q_ref[:, sl] = (_rope_chunk(qc, cos, sin_signed, first_half) * (HEAD_DIM ** -0.5)).astype(BF16)
    kt = lax.dot_general(wkt_ref[...], h, NT_DIMS, preferred_element_type=F32)
    cos_t, sin_t = cost_ref[...], sint_ref[...]
    half = HEAD_DIM // 2
    for g in range(N_SUB):
        x1 = kt[g * HEAD_DIM:g * HEAD_DIM + half, :]
        x2 = kt[g * HEAD_DIM + half:(g + 1) * HEAD_DIM, :]
        kt_ref[0, g * HEAD_DIM:g * HEAD_DIM + half, :] = x1 * cos_t - x2 * sin_t
        kt_ref[0, g * HEAD_DIM + half:(g + 1) * HEAD_DIM, :] = x2 * cos_t + x1 * sin_t
    v = jnp.dot(h, wv_ref[...], preferred_element_type=F32)
    for hh in range(N_HEADS):
        v_ref[0, pl.ds(hh, tm, stride=N_HEADS), :] = v[:, hh * LANES:(hh + 1) * LANES]


def _qkv_prompt(x, mods, wq, wkt, wv, cos, sin_signed, cos_t, sin_t, *, n_batch, seq, tm):
    tpb = seq // tm
    tok = lambda i: (i, 0)
    return pl.pallas_call(
        functools.partial(_qkv_prompt_kernel, tm=tm),
        grid=(n_batch * tpb,),
        in_specs=[pl.BlockSpec((tm, D_MODEL), tok),
                  _mod_spec(mods.shape[2], tpb),
                  _const_spec(wq.shape), _const_spec(wkt.shape), _const_spec(wv.shape),
                  pl.BlockSpec((tm, LANES), lambda i: (i % tpb, 0)),
                  pl.BlockSpec((tm, LANES), lambda i: (i % tpb, 0)),
                  pl.BlockSpec((HEAD_DIM // 2, tm), lambda i: (0, i % tpb)),
                  pl.BlockSpec((HEAD_DIM // 2, tm), lambda i: (0, i % tpb))],
        out_specs=[pl.BlockSpec((tm, D_MODEL), tok),
                   pl.BlockSpec((1, D_MODEL, tm), lambda i: (i // tpb, 0, i % tpb)),
                   pl.BlockSpec((1, tm * N_HEADS, LANES), lambda i: (i // tpb, i % tpb, 0))],
        out_shape=[jax.ShapeDtypeStruct((n_batch * seq, D_MODEL), BF16),
                   jax.ShapeDtypeStruct((n_batch, D_MODEL, seq), F32),
                   jax.ShapeDtypeStruct((n_batch, seq * N_HEADS, LANES), F32)],
        compiler_params=_params(1),
        name="qkv_rope_prompt",
    )(x, mods, wq, wkt, wv, cos, sin_signed, cos_t, sin_t)


def _lambda_value(lq1_ref, lk1_ref, lq2_ref, lk2_ref, lam_init):
    s1 = jnp.sum(lq1_ref[...] * lk1_ref[...], axis=1, keepdims=True)
    s2 = jnp.sum(lq2_ref[...] * lk2_ref[...], axis=1, keepdims=True)
    return jnp.exp(s1) - jnp.exp(s2) + lam_init


def _diff_norm(o0, o1, lam, subln_g, lam_init):
    a = o0 - lam * o1
    a = a * lax.rsqrt(jnp.mean(a * a, axis=-1, keepdims=True) + LN_EPS) * subln_g
    return a * (1.0 - lam_init)


def _online_softmax_step(s, v, m, l, acc):
    m_new = jnp.maximum(m, jnp.max(s, axis=1, keepdims=True))
    corr = jnp.exp(m - m_new)
    p = jnp.exp(s - m_new)
    l = l * corr + jnp.sum(p, axis=1, keepdims=True)
    acc = acc * corr + jnp.dot(p.astype(BF16), v, preferred_element_type=F32)
    return m_new, l, acc


def _flash_kernel(q_ref, kt_ref, v_ref, lq1_ref, lk1_ref, lq2_ref, lk2_ref, sg_ref, o_ref,
                  kb_ref, vb_ref, *, tq, n_chunks, lam_init):
    head = pl.program_id(1)
    qi = pl.program_id(2)

    @pl.when(qi == 0)
    def _():
        for c in range(n_chunks):
            for j in range(2):
                kb_ref[c, j] = kt_ref[0, j * HEAD_DIM:(j + 1) * HEAD_DIM, c * tq:(c + 1) * tq].astype(BF16)
            vb_ref[c] = v_ref[0, pl.ds(c * tq * N_HEADS + head, tq, stride=N_HEADS), :].astype(BF16)

    q_all = q_ref[...]
    rowi = lax.broadcasted_iota(jnp.int32, (tq, tq), 0)
    coli = lax.broadcasted_iota(jnp.int32, (tq, tq), 1)
    outs = []
    for j in range(2):
        q = q_all[:, j * HEAD_DIM:(j + 1) * HEAD_DIM]

        def body(kc, carry, q=q, j=j):
            s = jnp.dot(q, kb_ref[kc, j], preferred_element_type=F32)
            return _online_softmax_step(s, vb_ref[kc], *carry)

        init = (jnp.full((tq, 1), -jnp.inf, F32), jnp.zeros((tq, 1), F32),
                jnp.zeros((tq, 2 * HEAD_DIM), F32))
        m, l, acc = lax.fori_loop(0, qi, body, init)
        s = jnp.dot(q, kb_ref[qi, j], preferred_element_type=F32)
        s = jnp.where(coli <= rowi, s, -jnp.inf)
        m, l, acc = _online_softmax_step(s, vb_ref[qi], m, l, acc)
        outs.append(acc / l)
    lam = _lambda_value(lq1_ref, lk1_ref, lq2_ref, lk2_ref, lam_init)
    o_ref[...] = _diff_norm(outs[0], outs[1], lam, sg_ref[...], lam_init).astype(BF16)


def _flash(q, kt, v, lq1, lk1, lq2, lk2, subln_g, *, n_batch, seq, tq, lam_init):
    nq = seq // tq
    lam_spec = _const_spec((1, HEAD_DIM))
    return pl.pallas_call(
        functools.partial(_flash_kernel, tq=tq, n_chunks=nq, lam_init=lam_init),
        grid=(n_batch, N_HEADS, nq),
        in_specs=[pl.BlockSpec((tq, LANES), lambda b, h, i: (b * nq + i, h)),
                  pl.BlockSpec((1, 2 * HEAD_DIM, seq), lambda b, h, i: (b, h, 0)),
                  pl.BlockSpec((1, seq * N_HEADS, LANES), lambda b, h, i: (b, 0, 0)),
                  lam_spec, lam_spec, lam_spec, lam_spec,
                  _const_spec((1, 2 * HEAD_DIM))],
        out_specs=pl.BlockSpec((tq, LANES), lambda b, h, i: (b * nq + i, h)),
        out_shape=jax.ShapeDtypeStruct((n_batch * seq, D_MODEL), BF16),
        scratch_shapes=[pltpu.VMEM((nq, 2, HEAD_DIM, tq), BF16), pltpu.VMEM((nq, tq, 2 * HEAD_DIM), BF16)],
        compiler_params=_params(3),
        name="prompt_attn",
    )(q, kt, v, lq1, lk1, lq2, lk2, subln_g)


def _paged_kernel(pt_ref, q_ref, kn_ref, vn_ref, lq1_ref, lk1_ref, lq2_ref, lk2_ref, sg_ref, *rest,
                  pages_per_step, n_new, lam_init):
    k_refs = rest[:pages_per_step]
    v_refs = rest[pages_per_step:2 * pages_per_step]
    o_ref, qbd_ref, m_ref, l_ref, acc_ref = rest[2 * pages_per_step:]
    p = pl.program_id(1)
    hw = 2 * HEAD_DIM
    hr = 2 * n_new

    @pl.when(p == 0)
    def _():
        q = q_ref[0]
        for h in range(N_HEADS):
            qh = q[:, h * hw:(h + 1) * hw]
            lane_j = lax.broadcasted_iota(jnp.int32, qh.shape, 1) // HEAD_DIM
            for j in range(2):
                qbd_ref[h, j * n_new:(j + 1) * n_new, :] = jnp.where(lane_j == j, qh, jnp.zeros_like(qh))
        m_ref[...] = jnp.full(m_ref.shape, -jnp.inf, F32)
        l_ref[...] = jnp.zeros(l_ref.shape, F32)
        acc_ref[...] = jnp.zeros(acc_ref.shape, F32)

    def update(s, value_of):
        m_old = m_ref[...]
        m_new = jnp.maximum(m_old, jnp.max(s, axis=1, keepdims=True))
        corr = jnp.exp(m_old - m_new)
        pr = jnp.exp(s - m_new)
        l_ref[...] = l_ref[...] * corr + jnp.sum(pr, axis=1, keepdims=True)
        pb = pr.astype(BF16)
        pvs = []
        for h in range(N_HEADS):
            pv = None
            for i in range(s.shape[1] // PAGE_SIZE):
                part = jnp.dot(pb[h * hr:(h + 1) * hr, i * PAGE_SIZE:(i + 1) * PAGE_SIZE], value_of(h, i),
                               preferred_element_type=F32)
                pv = part if pv is None else pv + part
            pvs.append(pv)
        acc_ref[...] = acc_ref[...] * corr + jnp.concatenate(pvs, axis=0)
        m_ref[...] = m_new

    s = jnp.concatenate(
        [jnp.concatenate(
            [jnp.dot(qbd_ref[h], kr[0, 0, h * hw:(h + 1) * hw, :].astype(BF16), preferred_element_type=F32)
             for kr in k_refs], axis=1) for h in range(N_HEADS)], axis=0)
    update(s, lambda h, i: v_refs[i][0, 0, pl.ds(h, PAGE_SIZE, stride=N_HEADS), :].astype(BF16))

    @pl.when(p == pl.num_programs(1) - 1)
    def _():
        s_new = jnp.concatenate(
            [lax.dot_general(qbd_ref[h], kn_ref[0, :, h * hw:(h + 1) * hw].astype(BF16), NT_DIMS,
                             preferred_element_type=F32) for h in range(N_HEADS)], axis=0)
        tok = lax.broadcasted_iota(jnp.int32, s_new.shape, 0) % n_new
        key = lax.broadcasted_iota(jnp.int32, s_new.shape, 1)
        s_new = jnp.where((key < n_new) & (key <= tok), s_new, -jnp.inf)
        update(s_new, lambda h, i: vn_ref[0, :, h * hw:(h + 1) * hw].astype(BF16))
        o = acc_ref[...] / l_ref[...]
        lam = _lambda_value(lq1_ref, lk1_ref, lq2_ref, lk2_ref, lam_init)
        for h in range(N_HEADS):
            o0 = o[h * hr:h * hr + n_new]
            o1 = o[h * hr + n_new:(h + 1) * hr]
            o_ref[0, :, h * hw:(h + 1) * hw] = _diff_norm(o0, o1, lam, sg_ref[...], lam_init).astype(BF16)


def _paged(page_table, q, k_new, v_new, cache_kt, cache_v, lq1, lk1, lq2, lk2, subln_g, *,
           layer, pages_per_step, lam_init):
    n_batch, n_pages = page_table.shape
    n_new = q.shape[1]
    steps = n_pages // pages_per_step
    lam_spec = pl.BlockSpec((1, HEAD_DIM), lambda n, p, pt: (0, 0))

    def kpage_spec(i):
        return pl.BlockSpec((1, 1, D_MODEL, PAGE_SIZE),
                            lambda n, p, pt: (layer, pt[n * n_pages + p * pages_per_step + i], 0, 0))

    def vpage_spec(i):
        return pl.BlockSpec((1, 1, PAGE_SIZE * N_HEADS, 2 * HEAD_DIM),
                            lambda n, p, pt: (layer, pt[n * n_pages + p * pages_per_step + i], 0, 0))

    per_batch = lambda n, p, pt: (n, 0, 0)
    grid_spec = pltpu.PrefetchScalarGridSpec(
        num_scalar_prefetch=1,
        grid=(n_batch, steps),
        in_specs=[pl.BlockSpec((1, n_new, D_MODEL), per_batch),
                  pl.BlockSpec((1, PAGE_SIZE, D_MODEL), per_batch),
                  pl.BlockSpec((1, PAGE_SIZE, D_MODEL), per_batch),
                  lam_spec, lam_spec, lam_spec, lam_spec,
                  pl.BlockSpec((1, 2 * HEAD_DIM), lambda n, p, pt: (0, 0))]
                 + [kpage_spec(i) for i in range(pages_per_step)]
                 + [vpage_spec(i) for i in range(pages_per_step)],
        out_specs=pl.BlockSpec((1, n_new, D_MODEL), per_batch),
        scratch_shapes=[pltpu.VMEM((N_HEADS, 2 * n_new, 2 * HEAD_DIM), BF16),
                        pltpu.VMEM((N_SUB * n_new, 1), F32),
                        pltpu.VMEM((N_SUB * n_new, 1), F32),
                        pltpu.VMEM((N_SUB * n_new, 2 * HEAD_DIM), F32)],
    )
    return pl.pallas_call(
        functools.partial(_paged_kernel, pages_per_step=pages_per_step, n_new=n_new, lam_init=lam_init),
        grid_spec=grid_spec,
        out_shape=jax.ShapeDtypeStruct((n_batch, n_new, D_MODEL), BF16),
        compiler_params=_params(2),
        name="sample_attn",
    )(page_table.reshape(-1), q, k_new, v_new, lq1, lk1, lq2, lk2, subln_g,
      *([cache_kt] * pages_per_step), *([cache_v] * pages_per_step))


def _attn_out_kernel(a_ref, x_ref, mod_ref, wo_ref, g_ref, b_ref, wr_ref, x1_ref, h1_ref, lg_ref):
    y = jnp.dot(a_ref[...], wo_ref[...], preferred_element_type=F32)
    _mixer_epilogue(x_ref[...], y, mod_ref, g_ref, b_ref, wr_ref, x1_ref, h1_ref, lg_ref)


def _attn_out(a, x, mods, w_o, ln_g, ln_b, w_router, *, tm, tiles_per_batch):
    t_tok = x.shape[0]
    rows = mods.shape[2]
    tok = lambda i: (i, 0)
    return pl.pallas_call(
        _attn_out_kernel,
        grid=(t_tok // tm,),
        in_specs=[pl.BlockSpec((tm, D_MODEL), tok), pl.BlockSpec((tm, D_MODEL), tok),
                  _mod_spec(rows, tiles_per_batch), _const_spec(w_o.shape),
                  _const_spec((1, D_MODEL)), _const_spec((1, D_MODEL)),
                  _const_spec((D_MODEL, N_EXPERTS))],
        out_specs=[pl.BlockSpec((tm, D_MODEL), tok), pl.BlockSpec((tm, D_MODEL), tok),
                   pl.BlockSpec((tm, N_EXPERTS), tok)],
        out_shape=[jax.ShapeDtypeStruct((t_tok, D_MODEL), F32),
                   jax.ShapeDtypeStruct((t_tok, D_MODEL), BF16),
                   jax.ShapeDtypeStruct((t_tok, N_EXPERTS), F32)],
        compiler_params=_params(1),
        name="attn_out",
    )(a, x, mods, w_o, ln_g, ln_b, w_router)


def _rope_tables(pos):
    half = HEAD_DIM // 2
    inv_freq = ROPE_THETA ** (-jnp.arange(half, dtype=F32) * 2.0 / HEAD_DIM)
    ang = pos[:, None] * inv_freq[None, :]
    cos, sin = jnp.cos(ang), jnp.sin(ang)
    reps = LANES // HEAD_DIM
    cos_t = jnp.tile(jnp.concatenate([cos, cos], axis=1), (1, reps))
    sin_t = jnp.tile(jnp.concatenate([-sin, sin], axis=1), (1, reps))
    return cos_t, sin_t, cos.T, sin.T


def kernel(x_prompt, x_sample, state_pool, cache_k, cache_v, page_table, c_prompt, c_sample, w_ada, b_ada, ln1_g, ln1_b, ln2_g, ln2_b, w_pool, pool_scale, w_qkv, lambda_q1, lambda_k1, lambda_q2, lambda_k2, subln_g, w_o, w_router, router_bias, w_gate, w_up, w_down, ws_gate, ws_up, ws_down):
    n_p, l_p, d = x_prompt.shape
    n_s, l_s, _ = x_sample.shape
    t_p, t_s = n_p * l_p, n_s * l_s
    tm_p = 512
    tpb_p = l_p // tm_p
    tm_moe = 1024
    row = lambda a: a.reshape(1, -1)

    mods = _ada(jnp.concatenate([c_prompt, c_sample], axis=0), w_ada, b_ada)
    xp = x_prompt.reshape(t_p, d)
    xs = x_sample.reshape(t_s, d)
    cache_kt = jnp.transpose(cache_k, (0, 1, 3, 4, 5, 2)).reshape(cache_k.shape[0], cache_k.shape[1], d,
                                                                   PAGE_SIZE)
    cache_vr = cache_v.reshape(cache_v.shape[0], cache_v.shape[1], PAGE_SIZE * N_HEADS, 2 * HEAD_DIM)
    cos_p, sin_p, cos_pt, sin_pt = _rope_tables(jnp.arange(l_p, dtype=F32))
    cos_s, sin_s, _, _ = _rope_tables(jnp.arange(l_s, dtype=F32) + PAST_LEN)
    cos_s, sin_s = jnp.tile(cos_s, (n_s, 1)), jnp.tile(sin_s, (n_s, 1))

    pool_p, pool_s, kp_l, vp_l, ks_l, vs_l = [], [], [], [], [], []
    for i in range(DEPTH):
        j = i // 2
        mods_p = mods[i, :, :n_p].reshape(6, n_p, 1, d)
        mods_s_batch = mods[i, :, n_p:].reshape(6, n_s, 1, d)
        mods_s_tok = jnp.repeat(mods[i, :, n_p:], l_s, axis=1).reshape(6, 1, t_s, d)
        g1, b1, g2, b2 = row(ln1_g[i]), row(ln1_b[i]), row(ln2_g[i]), row(ln2_b[i])
        if i % 2 == 0:
            wp = w_pool[j].astype(BF16)
            ps = row(pool_scale[j])
            xp1, hp1, lgp, st_p = _pool_layer(xp, xp, mods_p, wp, ps, g1, b1, w_router[i], tm=tm_p,
                                              tiles_per_batch=tpb_p, start=0, halo_is_state=False)
            hist = jnp.pad(state_pool[j], ((0, 0), (POOL_HALO - state_pool.shape[2], 0), (0, 0)))
            xs1, hs1, lgs, st_s = _pool_layer(xs, hist, mods_s_batch, wp, ps, g1, b1, w_router[i], tm=l_s,
                                              tiles_per_batch=1, start=PAST_LEN, halo_is_state=True)
            keep = state_pool.shape[2]
            pool_p.append(st_p[:, POOL_HALO - keep:])
            pool_s.append(st_s[:, POOL_HALO - keep:])
        else:
            lam_init = 0.8 - 0.6 * math.exp(-0.3 * i)
            wq = w_qkv[j].astype(BF16)
            wo = w_o[j].astype(BF16)
            lams = (row(lambda_q1[j]), row(lambda_k1[j]), row(lambda_q2[j]), row(lambda_k2[j]))
            sg = row(subln_g[j])
            wkt = w_qkv[j][:, d:2 * d].T.astype(BF16)
            qp, kpt, vp = _qkv_prompt(xp, mods_p, wq[:, :d], wkt, wq[:, 2 * d:], cos_p, sin_p, cos_pt, sin_pt,
                                      n_batch=n_p, seq=l_p, tm=tm_p)
            ap = _flash(qp, kpt, vp, *lams, sg, n_batch=n_p, seq=l_p, tq=512, lam_init=lam_init)
            xp1, hp1, lgp = _attn_out(ap, xp, mods_p, wo, g1, b1, w_router[i], tm=tm_p,
                                      tiles_per_batch=tpb_p)
            qs, ks, vs = _qkv(xs, mods_s_tok, wq, cos_s, sin_s, tm=t_s, tiles_per_batch=1, table_tiles=1)
            pad_new = lambda a: jnp.pad(a.reshape(n_s, l_s, d), ((0, 0), (0, PAGE_SIZE - l_s), (0, 0)))
            a_s = _paged(page_table, qs.reshape(n_s, l_s, d), pad_new(ks), pad_new(vs), cache_kt, cache_vr,
                         *lams, sg, layer=j, pages_per_step=8, lam_init=lam_init)
            xs1, hs1, lgs = _attn_out(a_s.reshape(t_s, d), xs, mods_s_tok, wo, g1, b1, w_router[i],
                                      tm=t_s, tiles_per_batch=1)
            kp_l.append(kpt.reshape(n_p, N_HEADS, 2, HEAD_DIM, l_p).transpose(0, 4, 1, 2, 3))
            vp_l.append(vp.reshape(n_p, l_p, N_HEADS, 2 * HEAD_DIM))
            ks_l.append(ks.reshape(n_s, l_s, N_HEADS, 2, HEAD_DIM))
            vs_l.append(vs.reshape(n_s, l_s, N_HEADS, 2 * HEAD_DIM))
        bias = router_bias[i].reshape(N_EXPERTS, 1)
        gates_p = _router(lgp, bias, tm=256)
        gates_s = _router(lgs, bias, tm=256)
        experts = (w_gate, w_up, w_down, ws_gate, ws_up, ws_down)
        xp = _moe(hp1, gates_p, xp1, mods_p, *experts, g2, b2, layer=i, tm=tm_moe,
                  tiles_per_batch=l_p // tm_moe)
        xs = _moe(hs1, gates_s, xs1, mods_s_tok, *experts, g2, b2, layer=i, tm=t_s, tiles_per_batch=1)
    return (xp.reshape(n_p, l_p, d), xs.reshape(n_s, l_s, d), jnp.stack(pool_p), jnp.stack(pool_s),
            jnp.stack(kp_l), jnp.stack(vp_l), jnp.stack(ks_l), jnp.stack(vs_l))
```

```python
import functools
import math

import jax
import jax.numpy as jnp
from jax import lax
from jax.experimental import pallas as pl
from jax.experimental.pallas import tpu as pltpu

F32 = jnp.float32
BF16 = jnp.bfloat16
HIGHEST = lax.Precision.HIGHEST

D_MODEL = 1024
DEPTH = 2
PAST_LEN = 16384
PAGE_SIZE = 128
POOL_WINDOWS = (2, 4, 8, 16)
POOL_CH = D_MODEL // len(POOL_WINDOWS)
POOL_HALO = 16
HEAD_DIM = 64
N_HEADS = D_MODEL // (2 * HEAD_DIM)
N_SUB = 2 * N_HEADS
ROPE_THETA = 10000.0
N_EXPERTS = 64
TOP_K = 6
N_GROUPS = 8
GROUP_SIZE = N_EXPERTS // N_GROUPS
TOPK_GROUPS = 4
D_EXPERT = 256
ROUTED_SCALE = 2.5
ALPHA = (2 * DEPTH) ** 0.25
LN_EPS = 1e-5
LANES = 128
ROW_TILE = D_MODEL // LANES
GATHER_UNROLL = 8
VMEM_LIMIT = 56 * 1024 * 1024

NT_DIMS = (((1,), (1,)), ((), ()))


def _params(n_axes):
    return pltpu.CompilerParams(dimension_semantics=("arbitrary",) * n_axes,
                                vmem_limit_bytes=VMEM_LIMIT)


def _layer_norm(x, g, b):
    mu = jnp.mean(x, axis=-1, keepdims=True)
    xc = x - mu
    var = jnp.mean(xc * xc, axis=-1, keepdims=True)
    return xc * lax.rsqrt(var + LN_EPS) * g + b


def _mod_spec(rows, tiles_per_batch):
    return pl.BlockSpec((6, None, rows, D_MODEL), lambda i: (0, i // tiles_per_batch, 0, 0))


def _const_spec(shape):
    nd = len(shape)
    return pl.BlockSpec(shape, lambda *_: (0,) * nd)


def _ada_kernel(c_ref, w_ref, b_ref, o_ref):
    c = c_ref[...]
    s = c * jax.nn.sigmoid(c)
    o_ref[0, 0] = jnp.dot(s, w_ref[0], preferred_element_type=F32, precision=HIGHEST) + b_ref[0, 0]


def _ada(c_all, w_ada, b_ada):
    n = c_all.shape[0]
    depth = w_ada.shape[0]
    return pl.pallas_call(
        _ada_kernel,
        grid=(depth, 6),
        in_specs=[pl.BlockSpec((n, D_MODEL), lambda i, j: (0, 0)),
                  pl.BlockSpec((1, D_MODEL, D_MODEL), lambda i, j: (i, 0, j)),
                  pl.BlockSpec((1, 1, 1, D_MODEL), lambda i, j: (i, j, 0, 0))],
        out_specs=pl.BlockSpec((1, 1, n, D_MODEL), lambda i, j: (i, j, 0, 0)),
        out_shape=jax.ShapeDtypeStruct((depth, 6, n, D_MODEL), F32),
        compiler_params=_params(2),
        name="ada",
    )(c_all, w_ada, b_ada.reshape(depth, 6, 1, D_MODEL))


def _tiles_to_rows(ref, first_tok, n_tok):
    return jnp.concatenate(
        [ref[pl.ds(first_tok * ROW_TILE + c, n_tok, stride=ROW_TILE), :] for c in range(ROW_TILE)], axis=1)


def _rows_to_tiles(ref, first_tok, val):
    for c in range(ROW_TILE):
        ref[pl.ds(first_tok * ROW_TILE + c, val.shape[0], stride=ROW_TILE), :] = val[:, c * LANES:(c + 1) * LANES]


def _mixer_epilogue(x, y, mod_ref, g_ref, b_ref, wr_ref, x1_ref, h1_ref, lg_ref):
    x1 = _layer_norm(ALPHA * x + mod_ref[2] * y, g_ref[...], b_ref[...])
    h1 = x1 * (1.0 + mod_ref[4]) + mod_ref[3]
    x1_ref[...] = x1
    _rows_to_tiles(h1_ref, 0, h1)
    lg_ref[...] = jnp.dot(h1, wr_ref[...], preferred_element_type=F32, precision=HIGHEST)


def _pool_kernel(x_ref, halo_ref, mod_ref, wp_ref, ps_ref, g_ref, b_ref, wr_ref,
                 x1_ref, h1_ref, lg_ref, st_ref, *, tm, tiles_per_batch, start, halo_is_state):
    t = pl.program_id(0) % tiles_per_batch
    shift, scale = mod_ref[0], mod_ref[1]
    x = x_ref[...]
    h = x * (1.0 + scale) + shift
    if halo_is_state:
        halo = halo_ref[...]
    else:
        halo = halo_ref[...] * (1.0 + scale) + shift
        halo = jnp.where(t == 0, 0.0, halo)
    ext = jnp.concatenate([halo, h], axis=0)
    c = POOL_CH
    s2 = ext + pltpu.roll(ext, 1, 0)
    s4 = s2[:, c:] + pltpu.roll(s2[:, c:], 2, 0)
    s8 = s4[:, c:] + pltpu.roll(s4[:, c:], 4, 0)
    s16 = s8[:, c:] + pltpu.roll(s8[:, c:], 8, 0)
    wins = (s2[POOL_HALO:, :c], s4[POOL_HALO:, :c], s8[POOL_HALO:, :c], s16[POOL_HALO:, :])
    pos = start + t * tm + lax.broadcasted_iota(jnp.int32, (tm, c), 0)
    ys = []
    for g, w in enumerate(POOL_WINDOWS):
        cnt = jnp.minimum(pos + 1, w).astype(F32)
        pooled = wins[g] / cnt - h[:, g * c:(g + 1) * c]
        ys.append(jnp.dot(pooled.astype(BF16), wp_ref[g], preferred_element_type=F32))
    y = jnp.concatenate(ys, axis=1) * ps_ref[...]
    st_ref[...] = ext[tm:, :]
    _mixer_epilogue(x, y, mod_ref, g_ref, b_ref, wr_ref, x1_ref, h1_ref, lg_ref)


def _pool_layer(x, halo, mods, w_pool, pool_scale, ln_g, ln_b, w_router, *, tm, tiles_per_batch,
                start, halo_is_state):
    t_tok = x.shape[0]
    n_batch = t_tok // (tm * tiles_per_batch)
    rows = mods.shape[2]
    if halo_is_state:
        halo_spec = pl.BlockSpec((None, POOL_HALO, D_MODEL), lambda i: (i, 0, 0))
    else:
        halo_spec = pl.BlockSpec((POOL_HALO, D_MODEL),
                                 lambda i: (jnp.maximum(i * (tm // POOL_HALO) - 1, 0), 0))
    kern = functools.partial(_pool_kernel, tm=tm, tiles_per_batch=tiles_per_batch, start=start,
                             halo_is_state=halo_is_state)
    return pl.pallas_call(
        kern,
        grid=(t_tok // tm,),
        in_specs=[pl.BlockSpec((tm, D_MODEL), lambda i: (i, 0)),
                  halo_spec,
                  _mod_spec(rows, tiles_per_batch),
                  _const_spec(w_pool.shape),
                  _const_spec((1, D_MODEL)), _const_spec((1, D_MODEL)), _const_spec((1, D_MODEL)),
                  _const_spec((D_MODEL, N_EXPERTS))],
        out_specs=[pl.BlockSpec((tm, D_MODEL), lambda i: (i, 0)),
                   pl.BlockSpec((tm * ROW_TILE, LANES), lambda i: (i, 0)),
                   pl.BlockSpec((tm, N_EXPERTS), lambda i: (i, 0)),
                   pl.BlockSpec((None, POOL_HALO, D_MODEL), lambda i: (i // tiles_per_batch, 0, 0))],
        out_shape=[jax.ShapeDtypeStruct((t_tok, D_MODEL), F32),
                   jax.ShapeDtypeStruct((t_tok * ROW_TILE, LANES), F32),
                   jax.ShapeDtypeStruct((t_tok, N_EXPERTS), F32),
                   jax.ShapeDtypeStruct((n_batch, POOL_HALO, D_MODEL), F32)],
        compiler_params=_params(1),
        name="pool_mixer",
    )(x, halo, mods, w_pool, pool_scale, ln_g, ln_b, w_router)


def _first_max_onehot(vals, idx):
    m = vals[0]
    for v in vals[1:]:
        m = jnp.maximum(m, v)
    m = jnp.max(m, axis=0, keepdims=True)
    first = None
    for v, ix in zip(vals, idx):
        cand = jnp.min(jnp.where(v == m, ix, N_EXPERTS), axis=0, keepdims=True)
        first = cand if first is None else jnp.minimum(first, cand)
    return [ix == first for ix in idx]


def _router_kernel(lg_ref, bias_ref, gates_ref, *, tm):
    lg = lg_ref[...]
    lgt = jnp.concatenate([lg, jnp.zeros((tm, LANES - N_EXPERTS), F32)], axis=1).T
    bias = bias_ref[...]
    row = lax.broadcasted_iota(jnp.int32, (GROUP_SIZE, tm), 0)
    scores, choice, idx = [], [], []
    for g in range(N_GROUPS):
        sl = slice(g * GROUP_SIZE, (g + 1) * GROUP_SIZE)
        s = jax.nn.sigmoid(lgt[sl, :])
        scores.append(s)
        choice.append(s + bias[sl, :])
        idx.append(row + g * GROUP_SIZE)
    gscore = []
    for g in range(N_GROUPS):
        ch = choice[g]
        m1 = jnp.max(ch, axis=0, keepdims=True)
        first = jnp.min(jnp.where(ch == m1, row, GROUP_SIZE), axis=0, keepdims=True)
        m2 = jnp.max(jnp.where(row == first, -jnp.inf, ch), axis=0, keepdims=True)
        gscore.append(m1 + m2)
    masked = []
    for g in range(N_GROUPS):
        rank = jnp.zeros((1, tm), jnp.int32)
        for o in range(N_GROUPS):
            if o == g:
                continue
            beats = (gscore[o] > gscore[g]) | ((gscore[o] == gscore[g]) & (o < g))
            rank = rank + beats.astype(jnp.int32)
        masked.append(jnp.where(rank < TOPK_GROUPS, choice[g], -jnp.inf))
    sel = [jnp.zeros((GROUP_SIZE, tm), jnp.bool_) for _ in range(N_GROUPS)]
    for _ in range(TOP_K):
        hit = _first_max_onehot(masked, idx)
        sel = [s | h for s, h in zip(sel, hit)]
        masked = [jnp.where(h, -jnp.inf, v) for v, h in zip(masked, hit)]
    picked = [jnp.where(s, sc, 0.0) for s, sc in zip(sel, scores)]
    tot = picked[0]
    for p in picked[1:]:
        tot = tot + p
    tot = jnp.sum(tot, axis=0, keepdims=True)
    gates_ref[...] = jnp.concatenate([p / tot * ROUTED_SCALE for p in picked], axis=0)


def _router(logits, bias, *, tm):
    t_tok = logits.shape[0]
    return pl.pallas_call(
        functools.partial(_router_kernel, tm=tm),
        grid=(t_tok // tm,),
        in_specs=[pl.BlockSpec((tm, N_EXPERTS), lambda i: (i, 0)),
                  _const_spec((N_EXPERTS, 1))],
        out_specs=pl.BlockSpec((N_EXPERTS, tm), lambda i: (0, i)),
        out_shape=jax.ShapeDtypeStruct((N_EXPERTS, t_tok), F32),
        compiler_params=_params(1),
        name="router",
    )(logits, bias)


def _swiglu_hidden(h, wg, wu):
    a = jnp.dot(h, wg, preferred_element_type=F32)
    u = jnp.dot(h, wu, preferred_element_type=F32)
    return a * jax.nn.sigmoid(a) * u


def _moe_routed_kernel(cnt_ref, src_ref, dst_ref, gl_ref, h_ref, wg_ref, wu_ref, wd_ref, sg_ref, su_ref, sd_ref,
                       o_ref, acc_ref, xg_ref, yb_ref, *, tb, rm, sub):
    b = pl.program_id(0)
    e = pl.program_id(1)
    tile = lambda i: pl.ds(pl.multiple_of(i * ROW_TILE, ROW_TILE), ROW_TILE)
    rows_at = lambda off: pl.ds(pl.multiple_of(off, ROW_TILE), ROW_TILE)

    @pl.when(e == 0)
    def _():
        @pl.when(b == 0)
        def _():
            xg_ref[...] = jnp.zeros(xg_ref.shape, F32)

        sg, su, sd = sg_ref[0], su_ref[0], sd_ref[0]

        def shared(s, c):
            x = _tiles_to_rows(h_ref, s * sub, sub).astype(BF16)
            y = jnp.dot(_swiglu_hidden(x, sg, su).astype(BF16), sd, preferred_element_type=F32)
            _rows_to_tiles(acc_ref, s * sub, y)
            return c

        lax.fori_loop(0, tb // sub, shared, 0)
        acc_ref[tile(tb)] = jnp.zeros((ROW_TILE, LANES), F32)

    n = cnt_ref[b * N_EXPERTS + e]
    wg, wu, wd = wg_ref[0, 0], wu_ref[0, 0], wd_ref[0, 0]

    def chunk(j, carry):
        base = j * rm
        groups = pl.cdiv(jnp.minimum(n - base, rm), GATHER_UNROLL)

        def gather(i, c):
            for u in range(GATHER_UNROLL):
                r = i * GATHER_UNROLL + u
                xg_ref[tile(r)] = h_ref[rows_at(src_ref[0, 0, 0, base + r])]
            return c

        lax.fori_loop(0, groups, gather, 0)
        x = _tiles_to_rows(xg_ref, 0, rm).astype(BF16)
        y = jnp.dot(_swiglu_hidden(x, wg, wu).astype(BF16), wd, preferred_element_type=F32)
        _rows_to_tiles(yb_ref, 0, y)

        def scatter(i, c):
            rows = []
            for u in range(GATHER_UNROLL):
                r = i * GATHER_UNROLL + u
                rows.append((dst_ref[0, 0, 0, base + r], gl_ref[0, 0, 0, base + r], r))
            vals = [acc_ref[rows_at(t)] + g * yb_ref[tile(r)] for t, g, r in rows]
            for (t, g, r), v in zip(rows, vals):
                acc_ref[rows_at(t)] = v
            return c

        lax.fori_loop(0, groups, scatter, 0)
        return carry

    lax.fori_loop(0, pl.cdiv(n, rm), chunk, 0)

    @pl.when(e == N_EXPERTS - 1)
    def _():
        def emit(s, c):
            o_ref[pl.ds(pl.multiple_of(s * sub, 8), sub), :] = _tiles_to_rows(acc_ref, s * sub, sub)
            return c

        lax.fori_loop(0, tb // sub, emit, 0)


def _moe_routed(counts, src_lists, dst_lists, gate_lists, h_tiles, wg, wu, wd, sg, su, sd, *, layer, tb, rm, sub):
    n_blocks, _, cap = src_lists.shape
    as_blocks = lambda a: a.reshape(n_blocks, N_EXPERTS, 1, cap)
    expert = lambda b, e, cnt: (layer, e, 0, 0)
    shared = lambda b, e, cnt: (layer, 0, 0)
    lists = lambda b, e, cnt: (b, e, 0, 0)
    list_spec = pl.BlockSpec((1, 1, 1, cap), lists, memory_space=pltpu.SMEM)
    grid_spec = pltpu.PrefetchScalarGridSpec(
        num_scalar_prefetch=1,
        grid=(n_blocks, N_EXPERTS),
        in_specs=[list_spec, list_spec, list_spec,
                  pl.BlockSpec((tb * ROW_TILE, LANES), lambda b, e, cnt: (b, 0)),
                  pl.BlockSpec((1, 1, D_MODEL, D_EXPERT), expert),
                  pl.BlockSpec((1, 1, D_MODEL, D_EXPERT), expert),
                  pl.BlockSpec((1, 1, D_EXPERT, D_MODEL), expert),
                  pl.BlockSpec((1,) + sg.shape[1:], shared), pl.BlockSpec((1,) + su.shape[1:], shared),
                  pl.BlockSpec((1,) + sd.shape[1:], shared)],
        out_specs=pl.BlockSpec((tb, D_MODEL), lambda b, e, cnt: (b, 0), pipeline_mode=pl.Buffered(1)),
        scratch_shapes=[pltpu.VMEM(((tb + 1) * ROW_TILE, LANES), F32),
                        pltpu.VMEM((rm * ROW_TILE, LANES), F32),
                        pltpu.VMEM((rm * ROW_TILE, LANES), F32)],
    )
    return pl.pallas_call(
        functools.partial(_moe_routed_kernel, tb=tb, rm=rm, sub=sub),
        grid_spec=grid_spec,
        out_shape=jax.ShapeDtypeStruct((n_blocks * tb, D_MODEL), F32),
        compiler_params=_params(2),
        name="moe_routed",
    )(counts.reshape(-1), as_blocks(src_lists), as_blocks(dst_lists), as_blocks(gate_lists), h_tiles,
      wg, wu, wd, sg, su, sd)


def _moe_tiling(n_tok):
    n_blocks, n_sub = 8, 10
    tb, rm = n_tok // n_blocks, 256
    sub = tb // n_sub
    assert tb * n_blocks == n_tok and sub * n_sub == tb and sub % 16 == 0
    return tb, rm, sub


def _dispatch_lists(gates_t, *, tb, rm):
    n_blocks = gates_t.shape[1] // tb
    g = gates_t.reshape(N_EXPERTS, n_blocks, tb).transpose(1, 0, 2)
    sel = g > 0
    tok = lax.broadcasted_iota(jnp.int32, g.shape, 2)
    key, gate = lax.sort((jnp.where(sel, tok, tok + tb), g), dimension=2, num_keys=1)
    cap = -(-tb // rm) * rm
    pad = ((0, 0), (0, 0), (0, cap - tb))
    key = jnp.pad(key, pad, constant_values=tb)
    return (jnp.sum(sel, axis=2, dtype=jnp.int32), jnp.minimum(key, tb - 1) * ROW_TILE,
            jnp.minimum(key, tb) * ROW_TILE, jnp.pad(gate, pad))


def _moe_out_kernel(x_ref, y_ref, mod_ref, g_ref, b_ref, o_ref):
    o_ref[...] = _layer_norm(ALPHA * x_ref[...] + mod_ref[5] * y_ref[...], g_ref[...], b_ref[...])


def _moe_out(x, y, mods, ln_g, ln_b, *, tm, tiles_per_batch, y_first_tile):
    t_tok = x.shape[0]
    tok = lambda i: (i, 0)
    return pl.pallas_call(
        _moe_out_kernel,
        grid=(t_tok // tm,),
        in_specs=[pl.BlockSpec((tm, D_MODEL), tok),
                  pl.BlockSpec((tm, D_MODEL), lambda i: (i + y_first_tile, 0)),
                  _mod_spec(mods.shape[2], tiles_per_batch),
                  _const_spec((1, D_MODEL)), _const_spec((1, D_MODEL))],
        out_specs=pl.BlockSpec((tm, D_MODEL), tok),
        out_shape=jax.ShapeDtypeStruct((t_tok, D_MODEL), F32),
        compiler_params=_params(1),
        name="moe_out",
    )(x, y, mods, ln_g, ln_b)


def _rope_chunk(xc, cos, sin_signed, first_half):
    half = HEAD_DIM // 2
    partner = jnp.where(first_half, pltpu.roll(xc, LANES - half, 1), pltpu.roll(xc, half, 1))
    return xc * cos + partner * sin_signed


def _qkv_kernel(x_ref, mod_ref, w_ref, cos_ref, sin_ref, q_ref, k_ref, v_ref):
    h = (x_ref[...] * (1.0 + mod_ref[1]) + mod_ref[0]).astype(BF16)
    cos, sin_signed = cos_ref[...], sin_ref[...]
    lane = lax.broadcasted_iota(jnp.int32, cos.shape, 1)
    first_half = (lane % HEAD_DIM) < (HEAD_DIM // 2)
    n_chunks = D_MODEL // LANES
    for c in range(n_chunks):
        sl = slice(c * LANES, (c + 1) * LANES)
        qc = jnp.dot(h, w_ref[:, c * LANES:(c + 1) * LANES], preferred_element_type=F32)
        q_ref[:, sl] = (_rope_chunk(qc, cos, sin_signed, first_half) * (HEAD_DIM ** -0.5)).astype(BF16)
        kc = jnp.dot(h, w_ref[:, D_MODEL + c * LANES:D_MODEL + (c + 1) * LANES],
                     preferred_element_type=F32)
        k_ref[:, sl] = _rope_chunk(kc, cos, sin_signed, first_half)
    v_ref[...] = jnp.dot(h, w_ref[:, 2 * D_MODEL:], preferred_element_type=F32)


def _qkv(x, mods, w_qkv, cos, sin_signed, *, tm, tiles_per_batch, table_tiles):
    t_tok = x.shape[0]
    rows = mods.shape[2]
    tok = lambda i: (i, 0)
    return pl.pallas_call(
        _qkv_kernel,
        grid=(t_tok // tm,),
        in_specs=[pl.BlockSpec((tm, D_MODEL), tok),
                  _mod_spec(rows, tiles_per_batch),
                  _const_spec(w_qkv.shape),
                  pl.BlockSpec((tm, LANES), lambda i: (i % table_tiles, 0)),
                  pl.BlockSpec((tm, LANES), lambda i: (i % table_tiles, 0))],
        out_specs=[pl.BlockSpec((tm, D_MODEL), tok)] * 3,
        out_shape=[jax.ShapeDtypeStruct((t_tok, D_MODEL), BF16),
                   jax.ShapeDtypeStruct((t_tok, D_MODEL), F32),
                   jax.ShapeDtypeStruct((t_tok, D_MODEL), F32)],
        compiler_params=_params(1),
        name="qkv_rope",
    )(x, mods, w_qkv, cos, sin_signed)


def _qkv_prompt_kernel(x_ref, mod_ref, wq_ref, wkt_ref, wv_ref, cos_ref, sin_ref, cost_ref, sint_ref,
                       q_ref, kt_ref, v_ref, *, tm):
    h = (x_ref[...] * (1.0 + mod_ref[1]) + mod_ref[0]).astype(BF16)
    cos, sin_signed = cos_ref[...], sin_ref[...]
    lane = lax.broadcasted_iota(jnp.int32, cos.shape, 1)
    first_half = (lane % HEAD_DIM) < (HEAD_DIM // 2)
    for c in range(D_MODEL // LANES):
        sl = slice(c * LANES, (c + 1) * LANES)
        qc = jnp.dot(h, wq_ref[:, sl], preferred_element_type=F32)
        q_ref[:, sl] = (_rope_chunk(qc, cos, sin_signed, first_half) * (HEAD_DIM ** -0.5)).astype(BF16)
    kt = lax.dot_general(wkt_ref[...], h, NT_DIMS, preferred_element_type=F32)
    cos_t, sin_t = cost_ref[...], sint_ref[...]
    half = HEAD_DIM // 2
    for g in range(N_SUB):
        x1 = kt[g * HEAD_DIM:g * HEAD_DIM + half, :]
        x2 = kt[g * HEAD_DIM + half:(g + 1) * HEAD_DIM, :]
        kt_ref[0, g * HEAD_DIM:g * HEAD_DIM + half, :] = x1 * cos_t - x2 * sin_t
        kt_ref[0, g * HEAD_DIM + half:(g + 1) * HEAD_DIM, :] = x2 * cos_t + x1 * sin_t
    v = jnp.dot(h, wv_ref[...], preferred_element_type=F32)
    for hh in range(N_HEADS):
        v_ref[0, pl.ds(hh, tm, stride=N_HEADS), :] = v[:, hh * LANES:(hh + 1) * LANES]


def _qkv_prompt(x, mods, wq, wkt, wv, cos, sin_signed, cos_t, sin_t, *, n_batch, seq, tm):
    tpb = seq // tm
    tok = lambda i: (i, 0)
    return pl.pallas_call(
        functools.partial(_qkv_prompt_kernel, tm=tm),
        grid=(n_batch * tpb,),
        in_specs=[pl.BlockSpec((tm, D_MODEL), tok),
                  _mod_spec(mods.shape[2], tpb),
                  _const_spec(wq.shape), _const_spec(wkt.shape), _const_spec(wv.shape),
                  pl.BlockSpec((tm, LANES), lambda i: (i % tpb, 0)),
                  pl.BlockSpec((tm, LANES), lambda i: (i % tpb, 0)),
                  pl.BlockSpec((HEAD_DIM // 2, tm), lambda i: (0, i % tpb)),
                  pl.BlockSpec((HEAD_DIM // 2, tm), lambda i: (0, i % tpb))],
        out_specs=[pl.BlockSpec((tm, D_MODEL), tok),
                   pl.BlockSpec((1, D_MODEL, tm), lambda i: (i // tpb, 0, i % tpb)),
                   pl.BlockSpec((1, tm * N_HEADS, LANES), lambda i: (i // tpb, i % tpb, 0))],
        out_shape=[jax.ShapeDtypeStruct((n_batch * seq, D_MODEL), BF16),
                   jax.ShapeDtypeStruct((n_batch, D_MODEL, seq), F32),
                   jax.ShapeDtypeStruct((n_batch, seq * N_HEADS, LANES), F32)],
        compiler_params=_params(1),
        name="qkv_rope_prompt",
    )(x, mods, wq, wkt, wv, cos, sin_signed, cos_t, sin_t)


def _lambda_value(lq1_ref, lk1_ref, lq2_ref, lk2_ref, lam_init):
    s1 = jnp.sum(lq1_ref[...] * lk1_ref[...], axis=1, keepdims=True)
    s2 = jnp.sum(lq2_ref[...] * lk2_ref[...], axis=1, keepdims=True)
    return jnp.exp(s1) - jnp.exp(s2) + lam_init


def _diff_norm(o0, o1, lam, subln_g, lam_init):
    a = o0 - lam * o1
    a = a * lax.rsqrt(jnp.mean(a * a, axis=-1, keepdims=True) + LN_EPS) * subln_g
    return a * (1.0 - lam_init)


def _online_softmax_step(s, v, m, l, acc):
    m_new = jnp.maximum(m, jnp.max(s, axis=1, keepdims=True))
    corr = jnp.exp(m - m_new)
    p = jnp.exp(s - m_new)
    l = l * corr + jnp.sum(p, axis=1, keepdims=True)
    acc = acc * corr + jnp.dot(p.astype(BF16), v, preferred_element_type=F32)
    return m_new, l, acc


def _flash_kernel(q_ref, kt_ref, v_ref, lq1_ref, lk1_ref, lq2_ref, lk2_ref, sg_ref, o_ref,
                  kb_ref, vb_ref, *, tq, n_chunks, lam_init):
    head = pl.program_id(1)
    qi = pl.program_id(2)

    @pl.when(qi == 0)
    def _():
        for c in range(n_chunks):
            for j in range(2):
                kb_ref[c, j] = kt_ref[0, j * HEAD_DIM:(j + 1) * HEAD_DIM, c * tq:(c + 1) * tq].astype(BF16)
            vb_ref[c] = v_ref[0, pl.ds(c * tq * N_HEADS + head, tq, stride=N_HEADS), :].astype(BF16)

    q_all = q_ref[...]
    rowi = lax.broadcasted_iota(jnp.int32, (tq, tq), 0)
    coli = lax.broadcasted_iota(jnp.int32, (tq, tq), 1)
    outs = []
    for j in range(2):
        q = q_all[:, j * HEAD_DIM:(j + 1) * HEAD_DIM]

        def body(kc, carry, q=q, j=j):
            s = jnp.dot(q, kb_ref[kc, j], preferred_element_type=F32)
            return _online_softmax_step(s, vb_ref[kc], *carry)

        init = (jnp.full((tq, 1), -jnp.inf, F32), jnp.zeros((tq, 1), F32),
                jnp.zeros((tq, 2 * HEAD_DIM), F32))
        m, l, acc = lax.fori_loop(0, qi, body, init)
        s = jnp.dot(q, kb_ref[qi, j], preferred_element_type=F32)
        s = jnp.where(coli <= rowi, s, -jnp.inf)
        m, l, acc = _online_softmax_step(s, vb_ref[qi], m, l, acc)
        outs.append(acc / l)
    lam = _lambda_value(lq1_ref, lk1_ref, lq2_ref, lk2_ref, lam_init)
    o_ref[...] = _diff_norm(outs[0], outs[1], lam, sg_ref[...], lam_init).astype(BF16)


def _flash(q, kt, v, lq1, lk1, lq2, lk2, subln_g, *, n_batch, seq, tq, lam_init):
    nq = seq // tq
    lam_spec = _const_spec((1, HEAD_DIM))
    return pl.pallas_call(
        functools.partial(_flash_kernel, tq=tq, n_chunks=nq, lam_init=lam_init),
        grid=(n_batch, N_HEADS, nq),
        in_specs=[pl.BlockSpec((tq, LANES), lambda b, h, i: (b * nq + i, h)),
                  pl.BlockSpec((1, 2 * HEAD_DIM, seq), lambda b, h, i: (b, h, 0)),
                  pl.BlockSpec((1, seq * N_HEADS, LANES), lambda b, h, i: (b, 0, 0)),
                  lam_spec, lam_spec, lam_spec, lam_spec,
                  _const_spec((1, 2 * HEAD_DIM))],
        out_specs=pl.BlockSpec((tq, LANES), lambda b, h, i: (b * nq + i, h)),
        out_shape=jax.ShapeDtypeStruct((n_batch * seq, D_MODEL), BF16),
        scratch_shapes=[pltpu.VMEM((nq, 2, HEAD_DIM, tq), BF16), pltpu.VMEM((nq, tq, 2 * HEAD_DIM), BF16)],
        compiler_params=_params(3),
        name="prompt_attn",
    )(q, kt, v, lq1, lk1, lq2, lk2, subln_g)


def _paged_kernel(pt_ref, q_ref, kn_ref, vn_ref, lq1_ref, lk1_ref, lq2_ref, lk2_ref, sg_ref, *rest,
                  pages_per_step, n_new, lam_init):
    k_refs = rest[:pages_per_step]
    v_refs = rest[pages_per_step:2 * pages_per_step]
    o_ref, qbd_ref, m_ref, l_ref, acc_ref = rest[2 * pages_per_step:]
    p = pl.program_id(1)
    hw = 2 * HEAD_DIM
    hr = 2 * n_new

    @pl.when(p == 0)
    def _():
        q = q_ref[0]
        for h in range(N_HEADS):
            qh = q[:, h * hw:(h + 1) * hw]
            lane_j = lax.broadcasted_iota(jnp.int32, qh.shape, 1) // HEAD_DIM
            for j in range(2):
                qbd_ref[h, j * n_new:(j + 1) * n_new, :] = jnp.where(lane_j == j, qh, jnp.zeros_like(qh))
        m_ref[...] = jnp.full(m_ref.shape, -jnp.inf, F32)
        l_ref[...] = jnp.zeros(l_ref.shape, F32)
        acc_ref[...] = jnp.zeros(acc_ref.shape, F32)

    def update(s, value_of):
        m_old = m_ref[...]
        m_new = jnp.maximum(m_old, jnp.max(s, axis=1, keepdims=True))
        corr = jnp.exp(m_old - m_new)
        pr = jnp.exp(s - m_new)
        l_ref[...] = l_ref[...] * corr + jnp.sum(pr, axis=1, keepdims=True)
        pb = pr.astype(BF16)
        pvs = []
        for h in range(N_HEADS):
            pv = None
            for i in range(s.shape[1] // PAGE_SIZE):
                part = jnp.dot(pb[h * hr:(h + 1) * hr, i * PAGE_SIZE:(i + 1) * PAGE_SIZE], value_of(h, i),
                               preferred_element_type=F32)
                pv = part if pv is None else pv + part
            pvs.append(pv)
        acc_ref[...] = acc_ref[...] * corr + jnp.concatenate(pvs, axis=0)
        m_ref[...] = m_new

    s = jnp.concatenate(
        [jnp.concatenate(
            [jnp.dot(qbd_ref[h], kr[0, 0, h * hw:(h + 1) * hw, :].astype(BF16), preferred_element_type=F32)
             for kr in k_refs], axis=1) for h in range(N_HEADS)], axis=0)
    update(s, lambda h, i: v_refs[i][0, 0, pl.ds(h, PAGE_SIZE, stride=N_HEADS), :].astype(BF16))

    @pl.when(p == pl.num_programs(1) - 1)
    def _():
        s_new = jnp.concatenate(
            [lax.dot_general(qbd_ref[h], kn_ref[0, :, h * hw:(h + 1) * hw].astype(BF16), NT_DIMS,
                             preferred_element_type=F32) for h in range(N_HEADS)], axis=0)
        tok = lax.broadcasted_iota(jnp.int32, s_new.shape, 0) % n_new
        key = lax.broadcasted_iota(jnp.int32, s_new.shape, 1)
        s_new = jnp.where((key < n_new) & (key <= tok), s_new, -jnp.inf)
        update(s_new, lambda h, i: vn_ref[0, :, h * hw:(h + 1) * hw].astype(BF16))
        o = acc_ref[...] / l_ref[...]
        lam = _lambda_value(lq1_ref, lk1_ref, lq2_ref, lk2_ref, lam_init)
        for h in range(N_HEADS):
            o0 = o[h * hr:h * hr + n_new]
            o1 = o[h * hr + n_new:(h + 1) * hr]
            o_ref[0, :, h * hw:(h + 1) * hw] = _diff_norm(o0, o1, lam, sg_ref[...], lam_init).astype(BF16)


def _paged(page_table, q, k_new, v_new, cache_kt, cache_v, lq1, lk1, lq2, lk2, subln_g, *,
           layer, pages_per_step, lam_init):
    n_batch, n_pages = page_table.shape
    n_new = q.shape[1]
    steps = n_pages // pages_per_step
    lam_spec = pl.BlockSpec((1, HEAD_DIM), lambda n, p, pt: (0, 0))

    def kpage_spec(i):
        return pl.BlockSpec((1, 1, D_MODEL, PAGE_SIZE),
                            lambda n, p, pt: (layer, pt[n * n_pages + p * pages_per_step + i], 0, 0))

    def vpage_spec(i):
        return pl.BlockSpec((1, 1, PAGE_SIZE * N_HEADS, 2 * HEAD_DIM),
                            lambda n, p, pt: (layer, pt[n * n_pages + p * pages_per_step + i], 0, 0))

    per_batch = lambda n, p, pt: (n, 0, 0)
    grid_spec = pltpu.PrefetchScalarGridSpec(
        num_scalar_prefetch=1,
        grid=(n_batch, steps),
        in_specs=[pl.BlockSpec((1, n_new, D_MODEL), per_batch),
                  pl.BlockSpec((1, PAGE_SIZE, D_MODEL), per_batch),
                  pl.BlockSpec((1, PAGE_SIZE, D_MODEL), per_batch),
                  lam_spec, lam_spec, lam_spec, lam_spec,
                  pl.BlockSpec((1, 2 * HEAD_DIM), lambda n, p, pt: (0, 0))]
                 + [kpage_spec(i) for i in range(pages_per_step)]
                 + [vpage_spec(i) for i in range(pages_per_step)],
        out_specs=pl.BlockSpec((1, n_new, D_MODEL), per_batch),
        scratch_shapes=[pltpu.VMEM((N_HEADS, 2 * n_new, 2 * HEAD_DIM), BF16),
                        pltpu.VMEM((N_SUB * n_new, 1), F32),
                        pltpu.VMEM((N_SUB * n_new, 1), F32),
                        pltpu.VMEM((N_SUB * n_new, 2 * HEAD_DIM), F32)],
    )
    return pl.pallas_call(
        functools.partial(_paged_kernel, pages_per_step=pages_per_step, n_new=n_new, lam_init=lam_init),
        grid_spec=grid_spec,
        out_shape=jax.ShapeDtypeStruct((n_batch, n_new, D_MODEL), BF16),
        compiler_params=_params(2),
        name="sample_attn",
    )(page_table.reshape(-1), q, k_new, v_new, lq1, lk1, lq2, lk2, subln_g,
      *([cache_kt] * pages_per_step), *([cache_v] * pages_per_step))


def _attn_out_kernel(a_ref, x_ref, mod_ref, wo_ref, g_ref, b_ref, wr_ref, x1_ref, h1_ref, lg_ref):
    y = jnp.dot(a_ref[...], wo_ref[...], preferred_element_type=F32)
    _mixer_epilogue(x_ref[...], y, mod_ref, g_ref, b_ref, wr_ref, x1_ref, h1_ref, lg_ref)


def _attn_out(a, x, mods, w_o, ln_g, ln_b, w_router, *, tm, tiles_per_batch):
    t_tok = x.shape[0]
    rows = mods.shape[2]
    tok = lambda i: (i, 0)
    return pl.pallas_call(
        _attn_out_kernel,
        grid=(t_tok // tm,),
        in_specs=[pl.BlockSpec((tm, D_MODEL), tok), pl.BlockSpec((tm, D_MODEL), tok),
                  _mod_spec(rows, tiles_per_batch), _const_spec(w_o.shape),
                  _const_spec((1, D_MODEL)), _const_spec((1, D_MODEL)),
                  _const_spec((D_MODEL, N_EXPERTS))],
        out_specs=[pl.BlockSpec((tm, D_MODEL), tok), pl.BlockSpec((tm * ROW_TILE, LANES), tok),
                   pl.BlockSpec((tm, N_EXPERTS), tok)],
        out_shape=[jax.ShapeDtypeStruct((t_tok, D_MODEL), F32),
                   jax.ShapeDtypeStruct((t_tok * ROW_TILE, LANES), F32),
                   jax.ShapeDtypeStruct((t_tok, N_EXPERTS), F32)],
        compiler_params=_params(1),
        name="attn_out",
    )(a, x, mods, w_o, ln_g, ln_b, w_router)


def _rope_tables(pos):
    half = HEAD_DIM // 2
    inv_freq = ROPE_THETA ** (-jnp.arange(half, dtype=F32) * 2.0 / HEAD_DIM)
    ang = pos[:, None] * inv_freq[None, :]
    cos, sin = jnp.cos(ang), jnp.sin(ang)
    reps = LANES // HEAD_DIM
    cos_t = jnp.tile(jnp.concatenate([cos, cos], axis=1), (1, reps))
    sin_t = jnp.tile(jnp.concatenate([-sin, sin], axis=1), (1, reps))
    return cos_t, sin_t, cos.T, sin.T


def kernel(x_prompt, x_sample, state_pool, cache_k, cache_v, page_table, c_prompt, c_sample, w_ada, b_ada, ln1_g, ln1_b, ln2_g, ln2_b, w_pool, pool_scale, w_qkv, lambda_q1, lambda_k1, lambda_q2, lambda_k2, subln_g, w_o, w_router, router_bias, w_gate, w_up, w_down, ws_gate, ws_up, ws_down):
    n_p, l_p, d = x_prompt.shape
    n_s, l_s, _ = x_sample.shape
    t_p, t_s = n_p * l_p, n_s * l_s
    tm_p = 512
    tpb_p = l_p // tm_p
    tm_route = 256
    tb_moe, rm_moe, sub_moe = _moe_tiling(t_p + t_s)
    assert t_p % t_s == 0 and (t_p + t_s) % tm_route == 0
    row = lambda a: a.reshape(1, -1)
    experts = tuple(w.astype(BF16) for w in (w_gate, w_up, w_down, ws_gate, ws_up, ws_down))

    mods = _ada(jnp.concatenate([c_prompt, c_sample], axis=0), w_ada, b_ada)
    xp = x_prompt.reshape(t_p, d)
    xs = x_sample.reshape(t_s, d)
    cache_kt = jnp.transpose(cache_k, (0, 1, 3, 4, 5, 2)).reshape(cache_k.shape[0], cache_k.shape[1], d,
                                                                   PAGE_SIZE)
    cache_vr = cache_v.reshape(cache_v.shape[0], cache_v.shape[1], PAGE_SIZE * N_HEADS, 2 * HEAD_DIM)
    cos_p, sin_p, cos_pt, sin_pt = _rope_tables(jnp.arange(l_p, dtype=F32))
    cos_s, sin_s, _, _ = _rope_tables(jnp.arange(l_s, dtype=F32) + PAST_LEN)
    cos_s, sin_s = jnp.tile(cos_s, (n_s, 1)), jnp.tile(sin_s, (n_s, 1))

    pool_p, pool_s, kp_l, vp_l, ks_l, vs_l = [], [], [], [], [], []
    for i in range(DEPTH):
        j = i // 2
        mods_p = mods[i, :, :n_p].reshape(6, n_p, 1, d)
        mods_s_batch = mods[i, :, n_p:].reshape(6, n_s, 1, d)
        mods_s_tok = jnp.repeat(mods[i, :, n_p:], l_s, axis=1).reshape(6, 1, t_s, d)
        g1, b1, g2, b2 = row(ln1_g[i]), row(ln1_b[i]), row(ln2_g[i]), row(ln2_b[i])
        if i % 2 == 0:
            wp = w_pool[j].astype(BF16)
            ps = row(pool_scale[j])
            xp1, hp1, lgp, st_p = _pool_layer(xp, xp, mods_p, wp, ps, g1, b1, w_router[i], tm=tm_p,
                                              tiles_per_batch=tpb_p, start=0, halo_is_state=False)
            hist = jnp.pad(state_pool[j], ((0, 0), (POOL_HALO - state_pool.shape[2], 0), (0, 0)))
            xs1, hs1, lgs, st_s = _pool_layer(xs, hist, mods_s_batch, wp, ps, g1, b1, w_router[i], tm=l_s,
                                              tiles_per_batch=1, start=PAST_LEN, halo_is_state=True)
            keep = state_pool.shape[2]
            pool_p.append(st_p[:, POOL_HALO - keep:])
            pool_s.append(st_s[:, POOL_HALO - keep:])
        else:
            lam_init = 0.8 - 0.6 * math.exp(-0.3 * i)
            wq = w_qkv[j].astype(BF16)
            wo = w_o[j].astype(BF16)
            lams = (row(lambda_q1[j]), row(lambda_k1[j]), row(lambda_q2[j]), row(lambda_k2[j]))
            sg = row(subln_g[j])
            wkt = w_qkv[j][:, d:2 * d].T.astype(BF16)
            qp, kpt, vp = _qkv_prompt(xp, mods_p, wq[:, :d], wkt, wq[:, 2 * d:], cos_p, sin_p, cos_pt, sin_pt,
                                      n_batch=n_p, seq=l_p, tm=tm_p)
            ap = _flash(qp, kpt, vp, *lams, sg, n_batch=n_p, seq=l_p, tq=512, lam_init=lam_init)
            xp1, hp1, lgp = _attn_out(ap, xp, mods_p, wo, g1, b1, w_router[i], tm=tm_p,
                                      tiles_per_batch=tpb_p)
            qs, ks, vs = _qkv(xs, mods_s_tok, wq, cos_s, sin_s, tm=t_s, tiles_per_batch=1, table_tiles=1)
            pad_new = lambda a: jnp.pad(a.reshape(n_s, l_s, d), ((0, 0), (0, PAGE_SIZE - l_s), (0, 0)))
            a_s = _paged(page_table, qs.reshape(n_s, l_s, d), pad_new(ks), pad_new(vs), cache_kt, cache_vr,
                         *lams, sg, layer=j, pages_per_step=8, lam_init=lam_init)
            xs1, hs1, lgs = _attn_out(a_s.reshape(t_s, d), xs, mods_s_tok, wo, g1, b1, w_router[i],
                                      tm=t_s, tiles_per_batch=1)
            kp_l.append(kpt.reshape(n_p, N_HEADS, 2, HEAD_DIM, l_p).transpose(0, 4, 1, 2, 3))
            vp_l.append(vp.reshape(n_p, l_p, N_HEADS, 2 * HEAD_DIM))
            ks_l.append(ks.reshape(n_s, l_s, N_HEADS, 2, HEAD_DIM))
            vs_l.append(vs.reshape(n_s, l_s, N_HEADS, 2 * HEAD_DIM))
        gates_t = _router(jnp.concatenate([lgp, lgs], axis=0), router_bias[i].reshape(N_EXPERTS, 1), tm=tm_route)
        lists = _dispatch_lists(gates_t, tb=tb_moe, rm=rm_moe)
        y = _moe_routed(*lists, jnp.concatenate([hp1, hs1], axis=0), *experts, layer=i, tb=tb_moe, rm=rm_moe,
                        sub=sub_moe)
        xp = _moe_out(xp1, y, mods_p, g2, b2, tm=tm_p, tiles_per_batch=tpb_p, y_first_tile=0)
        xs = _moe_out(xs1, y, mods_s_tok, g2, b2, tm=t_s, tiles_per_batch=1, y_first_tile=t_p // t_s)
    return (xp.reshape(n_p, l_p, d), xs.reshape(n_s, l_s, d), jnp.stack(pool_p), jnp.stack(pool_s),
            jnp.stack(kp_l), jnp.stack(vp_l), jnp.stack(ks_l), jnp.stack(vs_l))
```

```python
import functools
import math

import jax
import jax.numpy as jnp
from jax import lax
from jax.experimental import pallas as pl
from jax.experimental.pallas import tpu as pltpu

F32 = jnp.float32
BF16 = jnp.bfloat16
HIGHEST = lax.Precision.HIGHEST

D_MODEL = 1024
DEPTH = 2
PAST_LEN = 16384
PAGE_SIZE = 128
POOL_WINDOWS = (2, 4, 8, 16)
POOL_CH = D_MODEL // len(POOL_WINDOWS)
POOL_HALO = 16
HEAD_DIM = 64
N_HEADS = D_MODEL // (2 * HEAD_DIM)
N_SUB = 2 * N_HEADS
ROPE_THETA = 10000.0
N_EXPERTS = 64
TOP_K = 6
N_GROUPS = 8
GROUP_SIZE = N_EXPERTS // N_GROUPS
TOPK_GROUPS = 4
D_EXPERT = 256
ROUTED_SCALE = 2.5
ALPHA = (2 * DEPTH) ** 0.25
LN_EPS = 1e-5
LANES = 128
ROW_TILE = D_MODEL // LANES
GATHER_UNROLL = 16
PICK_ROWS = 8
TOKEN_BITS = 12
VMEM_LIMIT = 56 * 1024 * 1024

NT_DIMS = (((1,), (1,)), ((), ()))


def _params(n_axes):
    return pltpu.CompilerParams(dimension_semantics=("arbitrary",) * n_axes,
                                vmem_limit_bytes=VMEM_LIMIT)


def _layer_norm(x, g, b):
    mu = jnp.mean(x, axis=-1, keepdims=True)
    xc = x - mu
    var = jnp.mean(xc * xc, axis=-1, keepdims=True)
    return xc * lax.rsqrt(var + LN_EPS) * g + b


def _mod_spec(rows, tiles_per_batch):
    return pl.BlockSpec((6, None, rows, D_MODEL), lambda i: (0, i // tiles_per_batch, 0, 0))


def _const_spec(shape):
    nd = len(shape)
    return pl.BlockSpec(shape, lambda *_: (0,) * nd)


def _ada_kernel(c_ref, w_ref, b_ref, o_ref):
    c = c_ref[...]
    s = c * jax.nn.sigmoid(c)
    o_ref[0, 0] = jnp.dot(s, w_ref[0], preferred_element_type=F32, precision=HIGHEST) + b_ref[0, 0]


def _ada(c_all, w_ada, b_ada):
    n = c_all.shape[0]
    depth = w_ada.shape[0]
    return pl.pallas_call(
        _ada_kernel,
        grid=(depth, 6),
        in_specs=[pl.BlockSpec((n, D_MODEL), lambda i, j: (0, 0)),
                  pl.BlockSpec((1, D_MODEL, D_MODEL), lambda i, j: (i, 0, j)),
                  pl.BlockSpec((1, 1, 1, D_MODEL), lambda i, j: (i, j, 0, 0))],
        out_specs=pl.BlockSpec((1, 1, n, D_MODEL), lambda i, j: (i, j, 0, 0)),
        out_shape=jax.ShapeDtypeStruct((depth, 6, n, D_MODEL), F32),
        compiler_params=_params(2),
        name="ada",
    )(c_all, w_ada, b_ada.reshape(depth, 6, 1, D_MODEL))


def _tiles_to_rows(ref, first_tok, n_tok):
    return jnp.concatenate(
        [ref[pl.ds(first_tok * ROW_TILE + c, n_tok, stride=ROW_TILE), :] for c in range(ROW_TILE)], axis=1)


def _rows_to_tiles(ref, first_tok, val):
    for c in range(ROW_TILE):
        ref[pl.ds(first_tok * ROW_TILE + c, val.shape[0], stride=ROW_TILE), :] = val[:, c * LANES:(c + 1) * LANES]


def _mixer_epilogue(x, y, mod_ref, g_ref, b_ref, wr_ref, x1_ref, h1_ref, lg_ref):
    x1 = _layer_norm(ALPHA * x + mod_ref[2] * y, g_ref[...], b_ref[...])
    h1 = x1 * (1.0 + mod_ref[4]) + mod_ref[3]
    x1_ref[...] = x1
    _rows_to_tiles(h1_ref, 0, h1)
    lg_ref[...] = jnp.dot(h1, wr_ref[...], preferred_element_type=F32, precision=HIGHEST)


def _pool_kernel(x_ref, halo_ref, mod_ref, wp_ref, ps_ref, g_ref, b_ref, wr_ref,
                 x1_ref, h1_ref, lg_ref, st_ref, *, tm, tiles_per_batch, start, halo_is_state):
    t = pl.program_id(0) % tiles_per_batch
    shift, scale = mod_ref[0], mod_ref[1]
    x = x_ref[...]
    h = x * (1.0 + scale) + shift
    if halo_is_state:
        halo = halo_ref[...]
    else:
        halo = halo_ref[...] * (1.0 + scale) + shift
        halo = jnp.where(t == 0, 0.0, halo)
    ext = jnp.concatenate([halo, h], axis=0)
    c = POOL_CH
    s2 = ext + pltpu.roll(ext, 1, 0)
    s4 = s2[:, c:] + pltpu.roll(s2[:, c:], 2, 0)
    s8 = s4[:, c:] + pltpu.roll(s4[:, c:], 4, 0)
    s16 = s8[:, c:] + pltpu.roll(s8[:, c:], 8, 0)
    wins = (s2[POOL_HALO:, :c], s4[POOL_HALO:, :c], s8[POOL_HALO:, :c], s16[POOL_HALO:, :])
    pos = start + t * tm + lax.broadcasted_iota(jnp.int32, (tm, c), 0)
    ys = []
    for g, w in enumerate(POOL_WINDOWS):
        cnt = jnp.minimum(pos + 1, w).astype(F32)
        pooled = wins[g] / cnt - h[:, g * c:(g + 1) * c]
        ys.append(jnp.dot(pooled.astype(BF16), wp_ref[g], preferred_element_type=F32))
    y = jnp.concatenate(ys, axis=1) * ps_ref[...]
    st_ref[...] = ext[tm:, :]
    _mixer_epilogue(x, y, mod_ref, g_ref, b_ref, wr_ref, x1_ref, h1_ref, lg_ref)


def _pool_layer(x, halo, mods, w_pool, pool_scale, ln_g, ln_b, w_router, *, tm, tiles_per_batch,
                start, halo_is_state):
    t_tok = x.shape[0]
    n_batch = t_tok // (tm * tiles_per_batch)
    rows = mods.shape[2]
    if halo_is_state:
        halo_spec = pl.BlockSpec((None, POOL_HALO, D_MODEL), lambda i: (i, 0, 0))
    else:
        halo_spec = pl.BlockSpec((POOL_HALO, D_MODEL),
                                 lambda i: (jnp.maximum(i * (tm // POOL_HALO) - 1, 0), 0))
    kern = functools.partial(_pool_kernel, tm=tm, tiles_per_batch=tiles_per_batch, start=start,
                             halo_is_state=halo_is_state)
    return pl.pallas_call(
        kern,
        grid=(t_tok // tm,),
        in_specs=[pl.BlockSpec((tm, D_MODEL), lambda i: (i, 0)),
                  halo_spec,
                  _mod_spec(rows, tiles_per_batch),
                  _const_spec(w_pool.shape),
                  _const_spec((1, D_MODEL)), _const_spec((1, D_MODEL)), _const_spec((1, D_MODEL)),
                  _const_spec((D_MODEL, N_EXPERTS))],
        out_specs=[pl.BlockSpec((tm, D_MODEL), lambda i: (i, 0)),
                   pl.BlockSpec((tm * ROW_TILE, LANES), lambda i: (i, 0)),
                   pl.BlockSpec((tm, N_EXPERTS), lambda i: (i, 0)),
                   pl.BlockSpec((None, POOL_HALO, D_MODEL), lambda i: (i // tiles_per_batch, 0, 0))],
        out_shape=[jax.ShapeDtypeStruct((t_tok, D_MODEL), F32),
                   jax.ShapeDtypeStruct((t_tok * ROW_TILE, LANES), F32),
                   jax.ShapeDtypeStruct((t_tok, N_EXPERTS), F32),
                   jax.ShapeDtypeStruct((n_batch, POOL_HALO, D_MODEL), F32)],
        compiler_params=_params(1),
        name="pool_mixer",
    )(x, halo, mods, w_pool, pool_scale, ln_g, ln_b, w_router)


def _first_max_onehot(vals, idx):
    m = vals[0]
    for v in vals[1:]:
        m = jnp.maximum(m, v)
    m = jnp.max(m, axis=0, keepdims=True)
    first = None
    for v, ix in zip(vals, idx):
        cand = jnp.min(jnp.where(v == m, ix, N_EXPERTS), axis=0, keepdims=True)
        first = cand if first is None else jnp.minimum(first, cand)
    return [ix == first for ix in idx], first


def _router_kernel(lg_ref, bias_ref, ids_ref, gates_ref, *, tm):
    lg = lg_ref[...]
    lgt = jnp.concatenate([lg, jnp.zeros((tm, LANES - N_EXPERTS), F32)], axis=1).T
    bias = bias_ref[...]
    row = lax.broadcasted_iota(jnp.int32, (GROUP_SIZE, tm), 0)
    scores, choice, idx = [], [], []
    for g in range(N_GROUPS):
        sl = slice(g * GROUP_SIZE, (g + 1) * GROUP_SIZE)
        s = jax.nn.sigmoid(lgt[sl, :])
        scores.append(s)
        choice.append(s + bias[sl, :])
        idx.append(row + g * GROUP_SIZE)
    gscore = []
    for g in range(N_GROUPS):
        ch = choice[g]
        m1 = jnp.max(ch, axis=0, keepdims=True)
        first = jnp.min(jnp.where(ch == m1, row, GROUP_SIZE), axis=0, keepdims=True)
        m2 = jnp.max(jnp.where(row == first, -jnp.inf, ch), axis=0, keepdims=True)
        gscore.append(m1 + m2)
    masked = []
    for g in range(N_GROUPS):
        rank = jnp.zeros((1, tm), jnp.int32)
        for o in range(N_GROUPS):
            if o == g:
                continue
            beats = (gscore[o] > gscore[g]) | ((gscore[o] == gscore[g]) & (o < g))
            rank = rank + beats.astype(jnp.int32)
        masked.append(jnp.where(rank < TOPK_GROUPS, choice[g], -jnp.inf))
    ids, picked = [], []
    for _ in range(TOP_K):
        hit, first = _first_max_onehot(masked, idx)
        masked = [jnp.where(h, -jnp.inf, v) for v, h in zip(masked, hit)]
        score = jnp.where(hit[0], scores[0], 0.0)
        for h, sc in zip(hit[1:], scores[1:]):
            score = score + jnp.where(h, sc, 0.0)
        ids.append(first)
        picked.append(jnp.sum(score, axis=0, keepdims=True))
    tot = picked[0]
    for p in picked[1:]:
        tot = tot + p
    fill = PICK_ROWS - TOP_K
    ids_ref[...] = jnp.concatenate(ids + [jnp.zeros((fill, tm), jnp.int32)], axis=0)
    gates_ref[...] = jnp.concatenate([p / tot * ROUTED_SCALE for p in picked] + [jnp.zeros((fill, tm), F32)],
                                     axis=0)


def _router(logits, bias, *, tm):
    t_tok = logits.shape[0]
    pick_spec = pl.BlockSpec((PICK_ROWS, tm), lambda i: (0, i))
    return pl.pallas_call(
        functools.partial(_router_kernel, tm=tm),
        grid=(t_tok // tm,),
        in_specs=[pl.BlockSpec((tm, N_EXPERTS), lambda i: (i, 0)),
                  _const_spec((N_EXPERTS, 1))],
        out_specs=[pick_spec, pick_spec],
        out_shape=[jax.ShapeDtypeStruct((PICK_ROWS, t_tok), jnp.int32),
                   jax.ShapeDtypeStruct((PICK_ROWS, t_tok), F32)],
        compiler_params=_params(1),
        name="router",
    )(logits, bias)


def _swiglu_hidden(h, wg, wu):
    a = jnp.dot(h, wg, preferred_element_type=F32)
    u = jnp.dot(h, wu, preferred_element_type=F32)
    return a * jax.nn.sigmoid(a) * u


def _moe_routed_kernel(off_ref, src_ref, gl_ref, h_ref, wg_ref, wu_ref, wd_ref, sg_ref, su_ref, sd_ref,
                       o_ref, acc_ref, xg_ref, yb_ref, *, tb, rm, sub):
    b = pl.program_id(0)
    e = pl.program_id(1)
    tile = lambda i: pl.ds(pl.multiple_of(i * ROW_TILE, ROW_TILE), ROW_TILE)
    rows_at = lambda off: pl.ds(pl.multiple_of(off, ROW_TILE), ROW_TILE)

    @pl.when(e == 0)
    def _():
        @pl.when(b == 0)
        def _():
            xg_ref[...] = jnp.zeros(xg_ref.shape, F32)

        sg, su, sd = sg_ref[0], su_ref[0], sd_ref[0]

        def shared(s, c):
            x = _tiles_to_rows(h_ref, s * sub, sub).astype(BF16)
            y = jnp.dot(_swiglu_hidden(x, sg, su).astype(BF16), sd, preferred_element_type=F32)
            _rows_to_tiles(acc_ref, s * sub, y)
            return c

        lax.fori_loop(0, tb // sub, shared, 0)
        acc_ref[tile(tb)] = jnp.zeros((ROW_TILE, LANES), F32)

    first = off_ref[b * (N_EXPERTS + 1) + e]
    n = off_ref[b * (N_EXPERTS + 1) + e + 1] - first

    def chunk(j, carry):
        base = first + j * rm
        m = jnp.minimum(n - j * rm, rm)
        groups = pl.cdiv(m, GATHER_UNROLL)

        def gather(i, c):
            for u in range(GATHER_UNROLL):
                r = i * GATHER_UNROLL + u
                xg_ref[tile(r)] = h_ref[rows_at(src_ref[0, 0, base + r])]
            return c

        lax.fori_loop(0, groups, gather, 0)
        x = _tiles_to_rows(xg_ref, 0, rm).astype(BF16)
        z = jnp.minimum(j, 0)
        y = jnp.dot(_swiglu_hidden(x, wg_ref[z, 0], wu_ref[z, 0]).astype(BF16), wd_ref[z, 0],
                    preferred_element_type=F32)
        _rows_to_tiles(yb_ref, 0, y)

        def scatter_group(i, partial):
            rows = []
            for u in range(GATHER_UNROLL):
                r = i * GATHER_UNROLL + u
                t, g = src_ref[0, 0, base + r], gl_ref[0, 0, base + r]
                if partial:
                    t, g = jnp.where(r < m, t, tb * ROW_TILE), jnp.where(r < m, g, 0.0)
                rows.append((t, g, r))
            vals = [acc_ref[rows_at(t)] + g * yb_ref[tile(r)] for t, g, r in rows]
            for (t, g, r), v in zip(rows, vals):
                acc_ref[rows_at(t)] = v

        def scatter_full(i, c):
            scatter_group(i, False)
            return c

        lax.fori_loop(0, m // GATHER_UNROLL, scatter_full, 0)

        @pl.when(m % GATHER_UNROLL != 0)
        def _():
            scatter_group(m // GATHER_UNROLL, True)

        return carry

    lax.fori_loop(0, pl.cdiv(n, rm), chunk, 0)

    @pl.when(e == N_EXPERTS - 1)
    def _():
        def emit(s, c):
            o_ref[pl.ds(pl.multiple_of(s * sub, 8), sub), :] = _tiles_to_rows(acc_ref, s * sub, sub)
            return c

        lax.fori_loop(0, tb // sub, emit, 0)


def _moe_routed(offsets, src_lists, gate_lists, h_tiles, wg, wu, wd, sg, su, sd, *, layer, tb, rm, sub):
    n_blocks, n_list = src_lists.shape
    as_blocks = lambda a: a.reshape(n_blocks, 1, n_list)
    expert = lambda b, e, off: (layer, e, 0, 0)
    shared = lambda b, e, off: (layer, 0, 0)
    list_spec = pl.BlockSpec((1, 1, n_list), lambda b, e, off: (b, 0, 0), memory_space=pltpu.SMEM)
    grid_spec = pltpu.PrefetchScalarGridSpec(
        num_scalar_prefetch=1,
        grid=(n_blocks, N_EXPERTS),
        in_specs=[list_spec, list_spec,
                  pl.BlockSpec((tb * ROW_TILE, LANES), lambda b, e, cnt: (b, 0)),
                  pl.BlockSpec((1, 1, D_MODEL, D_EXPERT), expert),
                  pl.BlockSpec((1, 1, D_MODEL, D_EXPERT), expert),
                  pl.BlockSpec((1, 1, D_EXPERT, D_MODEL), expert),
                  pl.BlockSpec((1,) + sg.shape[1:], shared), pl.BlockSpec((1,) + su.shape[1:], shared),
                  pl.BlockSpec((1,) + sd.shape[1:], shared)],
        out_specs=pl.BlockSpec((tb, D_MODEL), lambda b, e, cnt: (b, 0), pipeline_mode=pl.Buffered(1)),
        scratch_shapes=[pltpu.VMEM(((tb + 1) * ROW_TILE, LANES), F32),
                        pltpu.VMEM((rm * ROW_TILE, LANES), F32),
                        pltpu.VMEM((rm * ROW_TILE, LANES), F32)],
    )
    return pl.pallas_call(
        functools.partial(_moe_routed_kernel, tb=tb, rm=rm, sub=sub),
        grid_spec=grid_spec,
        out_shape=jax.ShapeDtypeStruct((n_blocks * tb, D_MODEL), F32),
        compiler_params=_params(2),
        name="moe_routed",
    )(offsets.reshape(-1), as_blocks(src_lists), as_blocks(gate_lists), h_tiles, wg, wu, wd, sg, su, sd)


def _moe_tiling(n_tok):
    n_blocks, n_sub = 8, 10
    tb, rm = n_tok // n_blocks, 256
    sub = tb // n_sub
    assert tb * n_blocks == n_tok and sub * n_sub == tb and sub % 16 == 0
    return tb, rm, sub


def _dispatch_lists(ids, gates, *, tb):
    assert tb <= 1 << TOKEN_BITS
    n_blocks = ids.shape[1] // tb
    per_block = lambda a: a[:TOP_K].reshape(TOP_K, n_blocks, tb).transpose(1, 0, 2).reshape(n_blocks, TOP_K * tb)
    eid, gate = per_block(ids), per_block(gates)
    tok = jnp.tile(jnp.arange(tb, dtype=jnp.int32), TOP_K)
    key, gate = lax.sort(((eid << TOKEN_BITS) | tok, gate), dimension=1, num_keys=1, is_stable=False)
    counts = jnp.sum(eid[:, None, :] == jnp.arange(N_EXPERTS, dtype=jnp.int32)[None, :, None], axis=2,
                     dtype=jnp.int32)
    offsets = jnp.concatenate([jnp.zeros((n_blocks, 1), jnp.int32), jnp.cumsum(counts, axis=1)], axis=1)
    tail = ((0, 0), (0, GATHER_UNROLL))
    return offsets, jnp.pad((key & ((1 << TOKEN_BITS) - 1)) * ROW_TILE, tail), jnp.pad(gate, tail)


def _moe_out_kernel(x_ref, y_ref, mod_ref, g_ref, b_ref, o_ref):
    o_ref[...] = _layer_norm(ALPHA * x_ref[...] + mod_ref[5] * y_ref[...], g_ref[...], b_ref[...])


def _moe_out(x, y, mods, ln_g, ln_b, *, tm, tiles_per_batch, y_first_tile):
    t_tok = x.shape[0]
    tok = lambda i: (i, 0)
    return pl.pallas_call(
        _moe_out_kernel,
        grid=(t_tok // tm,),
        in_specs=[pl.BlockSpec((tm, D_MODEL), tok),
                  pl.BlockSpec((tm, D_MODEL), lambda i: (i + y_first_tile, 0)),
                  _mod_spec(mods.shape[2], tiles_per_batch),
                  _const_spec((1, D_MODEL)), _const_spec((1, D_MODEL))],
        out_specs=pl.BlockSpec((tm, D_MODEL), tok),
        out_shape=jax.ShapeDtypeStruct((t_tok, D_MODEL), F32),
        compiler_params=_params(1),
        name="moe_out",
    )(x, y, mods, ln_g, ln_b)


def _rope_chunk(xc, cos, sin_signed, first_half):
    half = HEAD_DIM // 2
    partner = jnp.where(first_half, pltpu.roll(xc, LANES - half, 1), pltpu.roll(xc, half, 1))
    return xc * cos + partner * sin_signed


def _qkv_kernel(x_ref, mod_ref, w_ref, cos_ref, sin_ref, q_ref, k_ref, v_ref):
    h = (x_ref[...] * (1.0 + mod_ref[1]) + mod_ref[0]).astype(BF16)
    cos, sin_signed = cos_ref[...], sin_ref[...]
    lane = lax.broadcasted_iota(jnp.int32, cos.shape, 1)
    first_half = (lane % HEAD_DIM) < (HEAD_DIM // 2)
    n_chunks = D_MODEL // LANES
    for c in range(n_chunks):
        sl = slice(c * LANES, (c + 1) * LANES)
        qc = jnp.dot(h, w_ref[:, c * LANES:(c + 1) * LANES], preferred_element_type=F32)
        q_ref[:, sl] = (_rope_chunk(qc, cos, sin_signed, first_half) * (HEAD_DIM ** -0.5)).astype(BF16)
        kc = jnp.dot(h, w_ref[:, D_MODEL + c * LANES:D_MODEL + (c + 1) * LANES],
                     preferred_element_type=F32)
        k_ref[:, sl] = _rope_chunk(kc, cos, sin_signed, first_half)
    v_ref[...] = jnp.dot(h, w_ref[:, 2 * D_MODEL:], preferred_element_type=F32)


def _qkv(x, mods, w_qkv, cos, sin_signed, *, tm, tiles_per_batch, table_tiles):
    t_tok = x.shape[0]
    rows = mods.shape[2]
    tok = lambda i: (i, 0)
    return pl.pallas_call(
        _qkv_kernel,
        grid=(t_tok // tm,),
        in_specs=[pl.BlockSpec((tm, D_MODEL), tok),
                  _mod_spec(rows, tiles_per_batch),
                  _const_spec(w_qkv.shape),
                  pl.BlockSpec((tm, LANES), lambda i: (i % table_tiles, 0)),
                  pl.BlockSpec((tm, LANES), lambda i: (i % table_tiles, 0))],
        out_specs=[pl.BlockSpec((tm, D_MODEL), tok)] * 3,
        out_shape=[jax.ShapeDtypeStruct((t_tok, D_MODEL), BF16),
                   jax.ShapeDtypeStruct((t_tok, D_MODEL), F32),
                   jax.ShapeDtypeStruct((t_tok, D_MODEL), F32)],
        compiler_params=_params(1),
        name="qkv_rope",
    )(x, mods, w_qkv, cos, sin_signed)


def _qkv_prompt_kernel(x_ref, mod_ref, wq_ref, wkt_ref, wv_ref, cos_ref, sin_ref, cost_ref, sint_ref,
                       q_ref, kt_ref, v_ref, *, tm):
    h = (x_ref[...] * (1.0 + mod_ref[1]) + mod_ref[0]).astype(BF16)
    cos, sin_signed = cos_ref[...], sin_ref[...]
    lane = lax.broadcasted_iota(jnp.int32, cos.shape, 1)
    first_half = (lane % HEAD_DIM) < (HEAD_DIM // 2)
    for c in range(D_MODEL // LANES):
        sl = slice(c * LANES, (c + 1) * LANES)
        qc = jnp.dot(h, wq_ref[:, sl], preferred_element_type=F32)
        q_ref[:, sl] = (_rope_chunk(qc, cos, sin_signed, first_half) * (HEAD_DIM ** -0.5)).astype(BF16)
    kt = lax.dot_general(wkt_ref[...], h, NT_DIMS, preferred_element_type=F32)
    cos_t, sin_t = cost_ref[...], sint_ref[...]
    half = HEAD_DIM // 2
    for g in range(N_SUB):
        x1 = kt[g * HEAD_DIM:g * HEAD_DIM + half, :]
        x2 = kt[g * HEAD_DIM + half:(g + 1) * HEAD_DIM, :]
        kt_ref[0, g * HEAD_DIM:g * HEAD_DIM + half, :] = x1 * cos_t - x2 * sin_t
        kt_ref[0, g * HEAD_DIM + half:(g + 1) * HEAD_DIM, :] = x2 * cos_t + x1 * sin_t
    v = jnp.dot(h, wv_ref[...], preferred_element_type=F32)
    for hh in range(N_HEADS):
        v_ref[0, pl.ds(hh, tm, stride=N_HEADS), :] = v[:, hh * LANES:(hh + 1) * LANES]


def _qkv_prompt(x, mods, wq, wkt, wv, cos, sin_signed, cos_t, sin_t, *, n_batch, seq, tm):
    tpb = seq // tm
    tok = lambda i: (i, 0)
    return pl.pallas_call(
        functools.partial(_qkv_prompt_kernel, tm=tm),
        grid=(n_batch * tpb,),
        in_specs=[pl.BlockSpec((tm, D_MODEL), tok),
                  _mod_spec(mods.shape[2], tpb),
                  _const_spec(wq.shape), _const_spec(wkt.shape), _const_spec(wv.shape),
                  pl.BlockSpec((tm, LANES), lambda i: (i % tpb, 0)),
                  pl.BlockSpec((tm, LANES), lambda i: (i % tpb, 0)),
                  pl.BlockSpec((HEAD_DIM // 2, tm), lambda i: (0, i % tpb)),
                  pl.BlockSpec((HEAD_DIM // 2, tm), lambda i: (0, i % tpb))],
        out_specs=[pl.BlockSpec((tm, D_MODEL), tok),
                   pl.BlockSpec((1, D_MODEL, tm), lambda i: (i // tpb, 0, i % tpb)),
                   pl.BlockSpec((1, tm * N_HEADS, LANES), lambda i: (i // tpb, i % tpb, 0))],
        out_shape=[jax.ShapeDtypeStruct((n_batch * seq, D_MODEL), BF16),
                   jax.ShapeDtypeStruct((n_batch, D_MODEL, seq), F32),
                   jax.ShapeDtypeStruct((n_batch, seq * N_HEADS, LANES), F32)],
        compiler_params=_params(1),
        name="qkv_rope_prompt",
    )(x, mods, wq, wkt, wv, cos, sin_signed, cos_t, sin_t)


def _lambda_value(lq1_ref, lk1_ref, lq2_ref, lk2_ref, lam_init):
    s1 = jnp.sum(lq1_ref[...] * lk1_ref[...], axis=1, keepdims=True)
    s2 = jnp.sum(lq2_ref[...] * lk2_ref[...], axis=1, keepdims=True)
    return jnp.exp(s1) - jnp.exp(s2) + lam_init


def _diff_norm(o0, o1, lam, subln_g, lam_init):
    a = o0 - lam * o1
    a = a * lax.rsqrt(jnp.mean(a * a, axis=-1, keepdims=True) + LN_EPS) * subln_g
    return a * (1.0 - lam_init)


def _online_softmax_step(s, v, m, l, acc):
    m_new = jnp.maximum(m, jnp.max(s, axis=1, keepdims=True))
    corr = jnp.exp(m - m_new)
    p = jnp.exp(s - m_new)
    l = l * corr + jnp.sum(p, axis=1, keepdims=True)
    acc = acc * corr + jnp.dot(p.astype(BF16), v, preferred_element_type=F32)
    return m_new, l, acc


def _flash_kernel(q_ref, kt_ref, v_ref, lq1_ref, lk1_ref, lq2_ref, lk2_ref, sg_ref, o_ref,
                  kb_ref, vb_ref, *, tq, tk, n_chunks, lam_init):
    head = pl.program_id(1)
    qi = pl.program_id(2)

    @pl.when(qi == 0)
    def _():
        for c in range(n_chunks):
            for j in range(2):
                kb_ref[c, j] = kt_ref[0, j * HEAD_DIM:(j + 1) * HEAD_DIM, c * tk:(c + 1) * tk].astype(BF16)
            vb_ref[c] = v_ref[0, pl.ds(c * tk * N_HEADS + head, tk, stride=N_HEADS), :].astype(BF16)

    q_all = q_ref[...]
    last = (qi * tq) // tk
    row = qi * tq + lax.broadcasted_iota(jnp.int32, (tq, tk), 0)
    col = last * tk + lax.broadcasted_iota(jnp.int32, (tq, tk), 1)
    outs = []
    for j in range(2):
        q = q_all[:, j * HEAD_DIM:(j + 1) * HEAD_DIM]

        def body(kc, carry, q=q, j=j):
            s = jnp.dot(q, kb_ref[kc, j], preferred_element_type=F32)
            return _online_softmax_step(s, vb_ref[kc], *carry)

        init = (jnp.full((tq, 1), -jnp.inf, F32), jnp.zeros((tq, 1), F32),
                jnp.zeros((tq, 2 * HEAD_DIM), F32))
        m, l, acc = lax.fori_loop(0, last, body, init)
        s = jnp.dot(q, kb_ref[last, j], preferred_element_type=F32)
        s = jnp.where(col <= row, s, -jnp.inf)
        m, l, acc = _online_softmax_step(s, vb_ref[last], m, l, acc)
        outs.append(acc / l)
    lam = _lambda_value(lq1_ref, lk1_ref, lq2_ref, lk2_ref, lam_init)
    o_ref[...] = _diff_norm(outs[0], outs[1], lam, sg_ref[...], lam_init).astype(BF16)


def _flash(q, kt, v, lq1, lk1, lq2, lk2, subln_g, *, n_batch, seq, tq, tk, lam_init):
    assert tk % tq == 0 and seq % tk == 0
    nq, nk = seq // tq, seq // tk
    lam_spec = _const_spec((1, HEAD_DIM))
    return pl.pallas_call(
        functools.partial(_flash_kernel, tq=tq, tk=tk, n_chunks=nk, lam_init=lam_init),
        grid=(n_batch, N_HEADS, nq),
        in_specs=[pl.BlockSpec((tq, LANES), lambda b, h, i: (b * nq + i, h)),
                  pl.BlockSpec((1, 2 * HEAD_DIM, seq), lambda b, h, i: (b, h, 0)),
                  pl.BlockSpec((1, seq * N_HEADS, LANES), lambda b, h, i: (b, 0, 0)),
                  lam_spec, lam_spec, lam_spec, lam_spec,
                  _const_spec((1, 2 * HEAD_DIM))],
        out_specs=pl.BlockSpec((tq, LANES), lambda b, h, i: (b * nq + i, h)),
        out_shape=jax.ShapeDtypeStruct((n_batch * seq, D_MODEL), BF16),
        scratch_shapes=[pltpu.VMEM((nk, 2, HEAD_DIM, tk), BF16), pltpu.VMEM((nk, tk, 2 * HEAD_DIM), BF16)],
        compiler_params=_params(3),
        name="prompt_attn",
    )(q, kt, v, lq1, lk1, lq2, lk2, subln_g)


def _paged_kernel(pt_ref, q_ref, kn_ref, vn_ref, lq1_ref, lk1_ref, lq2_ref, lk2_ref, sg_ref, *rest,
                  pages_per_step, n_new, lam_init):
    k_refs = rest[:pages_per_step]
    v_refs = rest[pages_per_step:2 * pages_per_step]
    o_ref, qbd_ref, m_ref, l_ref, acc_ref = rest[2 * pages_per_step:]
    p = pl.program_id(1)
    hw = 2 * HEAD_DIM
    hr = 2 * n_new

    @pl.when(p == 0)
    def _():
        q = q_ref[0]
        for h in range(N_HEADS):
            qh = q[:, h * hw:(h + 1) * hw]
            lane_j = lax.broadcasted_iota(jnp.int32, qh.shape, 1) // HEAD_DIM
            for j in range(2):
                qbd_ref[h, j * n_new:(j + 1) * n_new, :] = jnp.where(lane_j == j, qh, jnp.zeros_like(qh))
        m_ref[...] = jnp.full(m_ref.shape, -jnp.inf, F32)
        l_ref[...] = jnp.zeros(l_ref.shape, F32)
        acc_ref[...] = jnp.zeros(acc_ref.shape, F32)

    def update(s, value_of):
        m_old = m_ref[...]
        m_new = jnp.maximum(m_old, jnp.max(s, axis=1, keepdims=True))
        corr = jnp.exp(m_old - m_new)
        pr = jnp.exp(s - m_new)
        l_ref[...] = l_ref[...] * corr + jnp.sum(pr, axis=1, keepdims=True)
        pb = pr.astype(BF16)
        pvs = []
        for h in range(N_HEADS):
            pv = None
            for i in range(s.shape[1] // PAGE_SIZE):
                part = jnp.dot(pb[h * hr:(h + 1) * hr, i * PAGE_SIZE:(i + 1) * PAGE_SIZE], value_of(h, i),
                               preferred_element_type=F32)
                pv = part if pv is None else pv + part
            pvs.append(pv)
        acc_ref[...] = acc_ref[...] * corr + jnp.concatenate(pvs, axis=0)
        m_ref[...] = m_new

    s = jnp.concatenate(
        [jnp.concatenate(
            [jnp.dot(qbd_ref[h], kr[0, 0, h * hw:(h + 1) * hw, :].astype(BF16), preferred_element_type=F32)
             for kr in k_refs], axis=1) for h in range(N_HEADS)], axis=0)
    update(s, lambda h, i: v_refs[i][0, 0, pl.ds(h, PAGE_SIZE, stride=N_HEADS), :].astype(BF16))

    @pl.when(p == pl.num_programs(1) - 1)
    def _():
        s_new = jnp.concatenate(
            [lax.dot_general(qbd_ref[h], kn_ref[0, :, h * hw:(h + 1) * hw].astype(BF16), NT_DIMS,
                             preferred_element_type=F32) for h in range(N_HEADS)], axis=0)
        tok = lax.broadcasted_iota(jnp.int32, s_new.shape, 0) % n_new
        key = lax.broadcasted_iota(jnp.int32, s_new.shape, 1)
        s_new = jnp.where((key < n_new) & (key <= tok), s_new, -jnp.inf)
        update(s_new, lambda h, i: vn_ref[0, :, h * hw:(h + 1) * hw].astype(BF16))
        o = acc_ref[...] / l_ref[...]
        lam = _lambda_value(lq1_ref, lk1_ref, lq2_ref, lk2_ref, lam_init)
        for h in range(N_HEADS):
            o0 = o[h * hr:h * hr + n_new]
            o1 = o[h * hr + n_new:(h + 1) * hr]
            o_ref[0, :, h * hw:(h + 1) * hw] = _diff_norm(o0, o1, lam, sg_ref[...], lam_init).astype(BF16)


def _paged(page_table, q, k_new, v_new, cache_kt, cache_v, lq1, lk1, lq2, lk2, subln_g, *,
           layer, pages_per_step, lam_init):
    n_batch, n_pages = page_table.shape
    n_new = q.shape[1]
    steps = n_pages // pages_per_step
    lam_spec = pl.BlockSpec((1, HEAD_DIM), lambda n, p, pt: (0, 0))

    def kpage_spec(i):
        return pl.BlockSpec((1, 1, D_MODEL, PAGE_SIZE),
                            lambda n, p, pt: (layer, pt[n * n_pages + p * pages_per_step + i], 0, 0))

    def vpage_spec(i):
        return pl.BlockSpec((1, 1, PAGE_SIZE * N_HEADS, 2 * HEAD_DIM),
                            lambda n, p, pt: (layer, pt[n * n_pages + p * pages_per_step + i], 0, 0))

    per_batch = lambda n, p, pt: (n, 0, 0)
    grid_spec = pltpu.PrefetchScalarGridSpec(
        num_scalar_prefetch=1,
        grid=(n_batch, steps),
        in_specs=[pl.BlockSpec((1, n_new, D_MODEL), per_batch),
                  pl.BlockSpec((1, PAGE_SIZE, D_MODEL), per_batch),
                  pl.BlockSpec((1, PAGE_SIZE, D_MODEL), per_batch),
                  lam_spec, lam_spec, lam_spec, lam_spec,
                  pl.BlockSpec((1, 2 * HEAD_DIM), lambda n, p, pt: (0, 0))]
                 + [kpage_spec(i) for i in range(pages_per_step)]
                 + [vpage_spec(i) for i in range(pages_per_step)],
        out_specs=pl.BlockSpec((1, n_new, D_MODEL), per_batch),
        scratch_shapes=[pltpu.VMEM((N_HEADS, 2 * n_new, 2 * HEAD_DIM), BF16),
                        pltpu.VMEM((N_SUB * n_new, 1), F32),
                        pltpu.VMEM((N_SUB * n_new, 1), F32),
                        pltpu.VMEM((N_SUB * n_new, 2 * HEAD_DIM), F32)],
    )
    return pl.pallas_call(
        functools.partial(_paged_kernel, pages_per_step=pages_per_step, n_new=n_new, lam_init=lam_init),
        grid_spec=grid_spec,
        out_shape=jax.ShapeDtypeStruct((n_batch, n_new, D_MODEL), BF16),
        compiler_params=_params(2),
        name="sample_attn",
    )(page_table.reshape(-1), q, k_new, v_new, lq1, lk1, lq2, lk2, subln_g,
      *([cache_kt] * pages_per_step), *([cache_v] * pages_per_step))


def _attn_out_kernel(a_ref, x_ref, mod_ref, wo_ref, g_ref, b_ref, wr_ref, x1_ref, h1_ref, lg_ref):
    y = jnp.dot(a_ref[...], wo_ref[...], preferred_element_type=F32)
    _mixer_epilogue(x_ref[...], y, mod_ref, g_ref, b_ref, wr_ref, x1_ref, h1_ref, lg_ref)


def _attn_out(a, x, mods, w_o, ln_g, ln_b, w_router, *, tm, tiles_per_batch):
    t_tok = x.shape[0]
    rows = mods.shape[2]
    tok = lambda i: (i, 0)
    return pl.pallas_call(
        _attn_out_kernel,
        grid=(t_tok // tm,),
        in_specs=[pl.BlockSpec((tm, D_MODEL), tok), pl.BlockSpec((tm, D_MODEL), tok),
                  _mod_spec(rows, tiles_per_batch), _const_spec(w_o.shape),
                  _const_spec((1, D_MODEL)), _const_spec((1, D_MODEL)),
                  _const_spec((D_MODEL, N_EXPERTS))],
        out_specs=[pl.BlockSpec((tm, D_MODEL), tok), pl.BlockSpec((tm * ROW_TILE, LANES), tok),
                   pl.BlockSpec((tm, N_EXPERTS), tok)],
        out_shape=[jax.ShapeDtypeStruct((t_tok, D_MODEL), F32),
                   jax.ShapeDtypeStruct((t_tok * ROW_TILE, LANES), F32),
                   jax.ShapeDtypeStruct((t_tok, N_EXPERTS), F32)],
        compiler_params=_params(1),
        name="attn_out",
    )(a, x, mods, w_o, ln_g, ln_b, w_router)


def _rope_tables(pos):
    half = HEAD_DIM // 2
    inv_freq = ROPE_THETA ** (-jnp.arange(half, dtype=F32) * 2.0 / HEAD_DIM)
    ang = pos[:, None] * inv_freq[None, :]
    cos, sin = jnp.cos(ang), jnp.sin(ang)
    reps = LANES // HEAD_DIM
    cos_t = jnp.tile(jnp.concatenate([cos, cos], axis=1), (1, reps))
    sin_t = jnp.tile(jnp.concatenate([-sin, sin], axis=1), (1, reps))
    return cos_t, sin_t, cos.T, sin.T


def kernel(x_prompt, x_sample, state_pool, cache_k, cache_v, page_table, c_prompt, c_sample, w_ada, b_ada, ln1_g, ln1_b, ln2_g, ln2_b, w_pool, pool_scale, w_qkv, lambda_q1, lambda_k1, lambda_q2, lambda_k2, subln_g, w_o, w_router, router_bias, w_gate, w_up, w_down, ws_gate, ws_up, ws_down):
    n_p, l_p, d = x_prompt.shape
    n_s, l_s, _ = x_sample.shape
    t_p, t_s = n_p * l_p, n_s * l_s
    tm_p = 512
    tpb_p = l_p // tm_p
    tm_route = 256
    tb_moe, rm_moe, sub_moe = _moe_tiling(t_p + t_s)
    assert t_p % t_s == 0 and (t_p + t_s) % tm_route == 0
    row = lambda a: a.reshape(1, -1)
    experts = tuple(w.astype(BF16) for w in (w_gate, w_up, w_down, ws_gate, ws_up, ws_down))

    mods = _ada(jnp.concatenate([c_prompt, c_sample], axis=0), w_ada, b_ada)
    xp = x_prompt.reshape(t_p, d)
    xs = x_sample.reshape(t_s, d)
    cache_kt = jnp.transpose(cache_k, (0, 1, 3, 4, 5, 2)).reshape(cache_k.shape[0], cache_k.shape[1], d,
                                                                   PAGE_SIZE)
    cache_vr = cache_v.reshape(cache_v.shape[0], cache_v.shape[1], PAGE_SIZE * N_HEADS, 2 * HEAD_DIM)
    cos_p, sin_p, cos_pt, sin_pt = _rope_tables(jnp.arange(l_p, dtype=F32))
    cos_s, sin_s, _, _ = _rope_tables(jnp.arange(l_s, dtype=F32) + PAST_LEN)
    cos_s, sin_s = jnp.tile(cos_s, (n_s, 1)), jnp.tile(sin_s, (n_s, 1))

    pool_p, pool_s, kp_l, vp_l, ks_l, vs_l = [], [], [], [], [], []
    for i in range(DEPTH):
        j = i // 2
        mods_p = mods[i, :, :n_p].reshape(6, n_p, 1, d)
        mods_s_batch = mods[i, :, n_p:].reshape(6, n_s, 1, d)
        mods_s_tok = jnp.repeat(mods[i, :, n_p:], l_s, axis=1).reshape(6, 1, t_s, d)
        g1, b1, g2, b2 = row(ln1_g[i]), row(ln1_b[i]), row(ln2_g[i]), row(ln2_b[i])
        if i % 2 == 0:
            wp = w_pool[j].astype(BF16)
            ps = row(pool_scale[j])
            xp1, hp1, lgp, st_p = _pool_layer(xp, xp, mods_p, wp, ps, g1, b1, w_router[i], tm=tm_p,
                                              tiles_per_batch=tpb_p, start=0, halo_is_state=False)
            hist = jnp.pad(state_pool[j], ((0, 0), (POOL_HALO - state_pool.shape[2], 0), (0, 0)))
            xs1, hs1, lgs, st_s = _pool_layer(xs, hist, mods_s_batch, wp, ps, g1, b1, w_router[i], tm=l_s,
                                              tiles_per_batch=1, start=PAST_LEN, halo_is_state=True)
            keep = state_pool.shape[2]
            pool_p.append(st_p[:, POOL_HALO - keep:])
            pool_s.append(st_s[:, POOL_HALO - keep:])
        else:
            lam_init = 0.8 - 0.6 * math.exp(-0.3 * i)
            wq = w_qkv[j].astype(BF16)
            wo = w_o[j].astype(BF16)
            lams = (row(lambda_q1[j]), row(lambda_k1[j]), row(lambda_q2[j]), row(lambda_k2[j]))
            sg = row(subln_g[j])
            wkt = w_qkv[j][:, d:2 * d].T.astype(BF16)
            qp, kpt, vp = _qkv_prompt(xp, mods_p, wq[:, :d], wkt, wq[:, 2 * d:], cos_p, sin_p, cos_pt, sin_pt,
                                      n_batch=n_p, seq=l_p, tm=tm_p)
            ap = _flash(qp, kpt, vp, *lams, sg, n_batch=n_p, seq=l_p, tq=512, tk=512, lam_init=lam_init)
            xp1, hp1, lgp = _attn_out(ap, xp, mods_p, wo, g1, b1, w_router[i], tm=tm_p,
                                      tiles_per_batch=tpb_p)
            qs, ks, vs = _qkv(xs, mods_s_tok, wq, cos_s, sin_s, tm=t_s, tiles_per_batch=1, table_tiles=1)
            pad_new = lambda a: jnp.pad(a.reshape(n_s, l_s, d), ((0, 0), (0, PAGE_SIZE - l_s), (0, 0)))
            a_s = _paged(page_table, qs.reshape(n_s, l_s, d), pad_new(ks), pad_new(vs), cache_kt, cache_vr,
                         *lams, sg, layer=j, pages_per_step=8, lam_init=lam_init)
            xs1, hs1, lgs = _attn_out(a_s.reshape(t_s, d), xs, mods_s_tok, wo, g1, b1, w_router[i],
                                      tm=t_s, tiles_per_batch=1)
            kp_l.append(kpt.reshape(n_p, N_HEADS, 2, HEAD_DIM, l_p).transpose(0, 4, 1, 2, 3))
            vp_l.append(vp.reshape(n_p, l_p, N_HEADS, 2 * HEAD_DIM))
            ks_l.append(ks.reshape(n_s, l_s, N_HEADS, 2, HEAD_DIM))
            vs_l.append(vs.reshape(n_s, l_s, N_HEADS, 2 * HEAD_DIM))
        picks = _router(jnp.concatenate([lgp, lgs], axis=0), router_bias[i].reshape(N_EXPERTS, 1), tm=tm_route)
        lists = _dispatch_lists(*picks, tb=tb_moe)
        y = _moe_routed(*lists, jnp.concatenate([hp1, hs1], axis=0), *experts, layer=i, tb=tb_moe, rm=rm_moe,
                        sub=sub_moe)
        xp = _moe_out(xp1, y, mods_p, g2, b2, tm=tm_p, tiles_per_batch=tpb_p, y_first_tile=0)
        xs = _moe_out(xs1, y, mods_s_tok, g2, b2, tm=t_s, tiles_per_batch=1, y_first_tile=t_p // t_s)
    return (xp.reshape(n_p, l_p, d), xs.reshape(n_s, l_s, d), jnp.stack(pool_p), jnp.stack(pool_s),
            jnp.stack(kp_l), jnp.stack(vp_l), jnp.stack(ks_l), jnp.stack(vs_l))
```

```python
import functools
import math

import jax
import jax.numpy as jnp
from jax import lax
from jax.experimental import pallas as pl
from jax.experimental.pallas import tpu as pltpu

F32 = jnp.float32
BF16 = jnp.bfloat16
HIGHEST = lax.Precision.HIGHEST

D_MODEL = 1024
DEPTH = 2
PAST_LEN = 16384
PAGE_SIZE = 128
POOL_WINDOWS = (2, 4, 8, 16)
POOL_CH = D_MODEL // len(POOL_WINDOWS)
POOL_HALO = 16
HEAD_DIM = 64
N_HEADS = D_MODEL // (2 * HEAD_DIM)
N_SUB = 2 * N_HEADS
ROPE_THETA = 10000.0
N_EXPERTS = 64
TOP_K = 6
N_GROUPS = 8
GROUP_SIZE = N_EXPERTS // N_GROUPS
TOPK_GROUPS = 4
D_EXPERT = 256
ROUTED_SCALE = 2.5
ALPHA = (2 * DEPTH) ** 0.25
LN_EPS = 1e-5
LANES = 128
ROW_TILE = D_MODEL // LANES
GATHER_UNROLL = 16
EXPERTS_PER_STEP = 2
PICK_ROWS = 8
TOKEN_BITS = 12
VMEM_LIMIT = 56 * 1024 * 1024

NT_DIMS = (((1,), (1,)), ((), ()))


def _params(n_axes):
    return pltpu.CompilerParams(dimension_semantics=("arbitrary",) * n_axes,
                                vmem_limit_bytes=VMEM_LIMIT)


def _layer_norm(x, g, b):
    mu = jnp.mean(x, axis=-1, keepdims=True)
    xc = x - mu
    var = jnp.mean(xc * xc, axis=-1, keepdims=True)
    return xc * lax.rsqrt(var + LN_EPS) * g + b


def _mod_spec(rows, tiles_per_batch):
    return pl.BlockSpec((6, None, rows, D_MODEL), lambda i: (0, i // tiles_per_batch, 0, 0))


def _const_spec(shape):
    nd = len(shape)
    return pl.BlockSpec(shape, lambda *_: (0,) * nd)


def _ada_kernel(c_ref, w_ref, b_ref, o_ref):
    c = c_ref[...]
    s = c * jax.nn.sigmoid(c)
    o_ref[0, 0] = jnp.dot(s, w_ref[0], preferred_element_type=F32, precision=HIGHEST) + b_ref[0, 0]


def _ada(c_all, w_ada, b_ada):
    n = c_all.shape[0]
    depth = w_ada.shape[0]
    return pl.pallas_call(
        _ada_kernel,
        grid=(depth, 6),
        in_specs=[pl.BlockSpec((n, D_MODEL), lambda i, j: (0, 0)),
                  pl.BlockSpec((1, D_MODEL, D_MODEL), lambda i, j: (i, 0, j)),
                  pl.BlockSpec((1, 1, 1, D_MODEL), lambda i, j: (i, j, 0, 0))],
        out_specs=pl.BlockSpec((1, 1, n, D_MODEL), lambda i, j: (i, j, 0, 0)),
        out_shape=jax.ShapeDtypeStruct((depth, 6, n, D_MODEL), F32),
        compiler_params=_params(2),
        name="ada",
    )(c_all, w_ada, b_ada.reshape(depth, 6, 1, D_MODEL))


def _tiles_to_rows(ref, first_tok, n_tok):
    return jnp.concatenate(
        [ref[pl.ds(first_tok * ROW_TILE + c, n_tok, stride=ROW_TILE), :] for c in range(ROW_TILE)], axis=1)


def _rows_to_tiles(ref, first_tok, val):
    for c in range(ROW_TILE):
        ref[pl.ds(first_tok * ROW_TILE + c, val.shape[0], stride=ROW_TILE), :] = val[:, c * LANES:(c + 1) * LANES]


def _mixer_epilogue(x, y, mod_ref, g_ref, b_ref, wr_ref, x1_ref, h1_ref, lg_ref):
    x1 = _layer_norm(ALPHA * x + mod_ref[2] * y, g_ref[...], b_ref[...])
    h1 = x1 * (1.0 + mod_ref[4]) + mod_ref[3]
    x1_ref[...] = x1
    _rows_to_tiles(h1_ref, 0, h1)
    lg_ref[...] = jnp.dot(h1, wr_ref[...], preferred_element_type=F32, precision=HIGHEST)


def _pool_kernel(x_ref, halo_ref, mod_ref, wp_ref, ps_ref, g_ref, b_ref, wr_ref,
                 x1_ref, h1_ref, lg_ref, st_ref, *, tm, tiles_per_batch, start, halo_is_state):
    t = pl.program_id(0) % tiles_per_batch
    shift, scale = mod_ref[0], mod_ref[1]
    x = x_ref[...]
    h = x * (1.0 + scale) + shift
    if halo_is_state:
        halo = halo_ref[...]
    else:
        halo = halo_ref[...] * (1.0 + scale) + shift
        halo = jnp.where(t == 0, 0.0, halo)
    ext = jnp.concatenate([halo, h], axis=0)
    c = POOL_CH
    s2 = ext + pltpu.roll(ext, 1, 0)
    s4 = s2[:, c:] + pltpu.roll(s2[:, c:], 2, 0)
    s8 = s4[:, c:] + pltpu.roll(s4[:, c:], 4, 0)
    s16 = s8[:, c:] + pltpu.roll(s8[:, c:], 8, 0)
    wins = (s2[POOL_HALO:, :c], s4[POOL_HALO:, :c], s8[POOL_HALO:, :c], s16[POOL_HALO:, :])
    pos = start + t * tm + lax.broadcasted_iota(jnp.int32, (tm, c), 0)
    ys = []
    for g, w in enumerate(POOL_WINDOWS):
        cnt = jnp.minimum(pos + 1, w).astype(F32)
        pooled = wins[g] / cnt - h[:, g * c:(g + 1) * c]
        ys.append(jnp.dot(pooled.astype(BF16), wp_ref[g], preferred_element_type=F32))
    y = jnp.concatenate(ys, axis=1) * ps_ref[...]
    st_ref[...] = ext[tm:, :]
    _mixer_epilogue(x, y, mod_ref, g_ref, b_ref, wr_ref, x1_ref, h1_ref, lg_ref)


def _pool_layer(x, halo, mods, w_pool, pool_scale, ln_g, ln_b, w_router, *, tm, tiles_per_batch,
                start, halo_is_state):
    t_tok = x.shape[0]
    n_batch = t_tok // (tm * tiles_per_batch)
    rows = mods.shape[2]
    if halo_is_state:
        halo_spec = pl.BlockSpec((None, POOL_HALO, D_MODEL), lambda i: (i, 0, 0))
    else:
        halo_spec = pl.BlockSpec((POOL_HALO, D_MODEL),
                                 lambda i: (jnp.maximum(i * (tm // POOL_HALO) - 1, 0), 0))
    kern = functools.partial(_pool_kernel, tm=tm, tiles_per_batch=tiles_per_batch, start=start,
                             halo_is_state=halo_is_state)
    return pl.pallas_call(
        kern,
        grid=(t_tok // tm,),
        in_specs=[pl.BlockSpec((tm, D_MODEL), lambda i: (i, 0)),
                  halo_spec,
                  _mod_spec(rows, tiles_per_batch),
                  _const_spec(w_pool.shape),
                  _const_spec((1, D_MODEL)), _const_spec((1, D_MODEL)), _const_spec((1, D_MODEL)),
                  _const_spec((D_MODEL, N_EXPERTS))],
        out_specs=[pl.BlockSpec((tm, D_MODEL), lambda i: (i, 0)),
                   pl.BlockSpec((tm * ROW_TILE, LANES), lambda i: (i, 0)),
                   pl.BlockSpec((tm, N_EXPERTS), lambda i: (i, 0)),
                   pl.BlockSpec((None, POOL_HALO, D_MODEL), lambda i: (i // tiles_per_batch, 0, 0))],
        out_shape=[jax.ShapeDtypeStruct((t_tok, D_MODEL), F32),
                   jax.ShapeDtypeStruct((t_tok * ROW_TILE, LANES), F32),
                   jax.ShapeDtypeStruct((t_tok, N_EXPERTS), F32),
                   jax.ShapeDtypeStruct((n_batch, POOL_HALO, D_MODEL), F32)],
        compiler_params=_params(1),
        name="pool_mixer",
    )(x, halo, mods, w_pool, pool_scale, ln_g, ln_b, w_router)


def _first_max_onehot(vals, idx):
    m = vals[0]
    for v in vals[1:]:
        m = jnp.maximum(m, v)
    m = jnp.max(m, axis=0, keepdims=True)
    first = None
    for v, ix in zip(vals, idx):
        cand = jnp.min(jnp.where(v == m, ix, N_EXPERTS), axis=0, keepdims=True)
        first = cand if first is None else jnp.minimum(first, cand)
    return [ix == first for ix in idx], first


def _router_kernel(lg_ref, bias_ref, ids_ref, gates_ref, *, tm):
    lg = lg_ref[...]
    lgt = jnp.concatenate([lg, jnp.zeros((tm, LANES - N_EXPERTS), F32)], axis=1).T
    bias = bias_ref[...]
    row = lax.broadcasted_iota(jnp.int32, (GROUP_SIZE, tm), 0)
    scores, choice, idx = [], [], []
    for g in range(N_GROUPS):
        sl = slice(g * GROUP_SIZE, (g + 1) * GROUP_SIZE)
        s = jax.nn.sigmoid(lgt[sl, :])
        scores.append(s)
        choice.append(s + bias[sl, :])
        idx.append(row + g * GROUP_SIZE)
    gscore = []
    for g in range(N_GROUPS):
        ch = choice[g]
        m1 = jnp.max(ch, axis=0, keepdims=True)
        first = jnp.min(jnp.where(ch == m1, row, GROUP_SIZE), axis=0, keepdims=True)
        m2 = jnp.max(jnp.where(row == first, -jnp.inf, ch), axis=0, keepdims=True)
        gscore.append(m1 + m2)
    masked = []
    for g in range(N_GROUPS):
        rank = jnp.zeros((1, tm), jnp.int32)
        for o in range(N_GROUPS):
            if o == g:
                continue
            beats = (gscore[o] > gscore[g]) | ((gscore[o] == gscore[g]) & (o < g))
            rank = rank + beats.astype(jnp.int32)
        masked.append(jnp.where(rank < TOPK_GROUPS, choice[g], -jnp.inf))
    ids, picked = [], []
    for _ in range(TOP_K):
        hit, first = _first_max_onehot(masked, idx)
        masked = [jnp.where(h, -jnp.inf, v) for v, h in zip(masked, hit)]
        score = jnp.where(hit[0], scores[0], 0.0)
        for h, sc in zip(hit[1:], scores[1:]):
            score = score + jnp.where(h, sc, 0.0)
        ids.append(first)
        picked.append(jnp.sum(score, axis=0, keepdims=True))
    tot = picked[0]
    for p in picked[1:]:
        tot = tot + p
    fill = PICK_ROWS - TOP_K
    ids_ref[...] = jnp.concatenate(ids + [jnp.zeros((fill, tm), jnp.int32)], axis=0)
    gates_ref[...] = jnp.concatenate([p / tot * ROUTED_SCALE for p in picked] + [jnp.zeros((fill, tm), F32)],
                                     axis=0)


def _router(logits, bias, *, tm):
    t_tok = logits.shape[0]
    pick_spec = pl.BlockSpec((PICK_ROWS, tm), lambda i: (0, i))
    return pl.pallas_call(
        functools.partial(_router_kernel, tm=tm),
        grid=(t_tok // tm,),
        in_specs=[pl.BlockSpec((tm, N_EXPERTS), lambda i: (i, 0)),
                  _const_spec((N_EXPERTS, 1))],
        out_specs=[pick_spec, pick_spec],
        out_shape=[jax.ShapeDtypeStruct((PICK_ROWS, t_tok), jnp.int32),
                   jax.ShapeDtypeStruct((PICK_ROWS, t_tok), F32)],
        compiler_params=_params(1),
        name="router",
    )(logits, bias)


def _swiglu_hidden(h, wg, wu):
    a = jnp.dot(h, wg, preferred_element_type=F32)
    u = jnp.dot(h, wu, preferred_element_type=F32)
    return a * jax.nn.sigmoid(a) * u


def _moe_routed_kernel(off_ref, src_ref, gl_ref, h_ref, wg_ref, wu_ref, wd_ref, sg_ref, su_ref, sd_ref,
                       o_ref, acc_ref, xg_ref, yb_ref, *, tb, rm, sub, epp):
    b = pl.program_id(0)
    step = pl.program_id(1)
    tile = lambda i: pl.ds(pl.multiple_of(i * ROW_TILE, ROW_TILE), ROW_TILE)
    rows_at = lambda off: pl.ds(pl.multiple_of(off, ROW_TILE), ROW_TILE)

    @pl.when(step == 0)
    def _():
        @pl.when(b == 0)
        def _():
            xg_ref[...] = jnp.zeros(xg_ref.shape, F32)

        sg, su, sd = sg_ref[0], su_ref[0], sd_ref[0]

        def shared(s, c):
            x = _tiles_to_rows(h_ref, s * sub, sub).astype(BF16)
            y = jnp.dot(_swiglu_hidden(x, sg, su).astype(BF16), sd, preferred_element_type=F32)
            _rows_to_tiles(acc_ref, s * sub, y)
            return c

        lax.fori_loop(0, tb // sub, shared, 0)
        acc_ref[tile(tb)] = jnp.zeros((ROW_TILE, LANES), F32)

    def route(k):
        e = step * epp + k
        first = off_ref[b * (N_EXPERTS + 1) + e]
        n = off_ref[b * (N_EXPERTS + 1) + e + 1] - first

        def chunk(j, carry):
            base = first + j * rm
            m = jnp.minimum(n - j * rm, rm)
            groups = pl.cdiv(m, GATHER_UNROLL)

            def gather(i, c):
                for u in range(GATHER_UNROLL):
                    r = i * GATHER_UNROLL + u
                    xg_ref[tile(r)] = h_ref[rows_at(src_ref[0, 0, base + r])]
                return c

            lax.fori_loop(0, groups, gather, 0)
            x = _tiles_to_rows(xg_ref, 0, rm).astype(BF16)
            y = jnp.dot(_swiglu_hidden(x, wg_ref[0, k], wu_ref[0, k]).astype(BF16), wd_ref[0, k],
                        preferred_element_type=F32)
            _rows_to_tiles(yb_ref, 0, y)

            def scatter_group(i, partial):
                rows = []
                for u in range(GATHER_UNROLL):
                    r = i * GATHER_UNROLL + u
                    t, g = src_ref[0, 0, base + r], gl_ref[0, 0, base + r]
                    if partial:
                        t, g = jnp.where(r < m, t, tb * ROW_TILE), jnp.where(r < m, g, 0.0)
                    rows.append((t, g, r))
                vals = [acc_ref[rows_at(t)] + g * yb_ref[tile(r)] for t, g, r in rows]
                for (t, g, r), v in zip(rows, vals):
                    acc_ref[rows_at(t)] = v

            def scatter_full(i, c):
                scatter_group(i, False)
                return c

            lax.fori_loop(0, m // GATHER_UNROLL, scatter_full, 0)

            @pl.when(m % GATHER_UNROLL != 0)
            def _():
                scatter_group(m // GATHER_UNROLL, True)

            return carry

        lax.fori_loop(0, pl.cdiv(n, rm), chunk, 0)

    for k in range(epp):
        route(k)

    @pl.when(step == pl.num_programs(1) - 1)
    def _():
        def emit(s, c):
            o_ref[pl.ds(pl.multiple_of(s * sub, 8), sub), :] = _tiles_to_rows(acc_ref, s * sub, sub)
            return c

        lax.fori_loop(0, tb // sub, emit, 0)


def _moe_routed(offsets, src_lists, gate_lists, h_tiles, wg, wu, wd, sg, su, sd, *, layer, tb, rm, sub):
    n_blocks, n_list = src_lists.shape
    as_blocks = lambda a: a.reshape(n_blocks, 1, n_list)
    expert = lambda b, e, off: (layer, e, 0, 0)
    shared = lambda b, e, off: (layer, 0, 0)
    list_spec = pl.BlockSpec((1, 1, n_list), lambda b, e, off: (b, 0, 0), memory_space=pltpu.SMEM)
    grid_spec = pltpu.PrefetchScalarGridSpec(
        num_scalar_prefetch=1,
        grid=(n_blocks, N_EXPERTS // EXPERTS_PER_STEP),
        in_specs=[list_spec, list_spec,
                  pl.BlockSpec((tb * ROW_TILE, LANES), lambda b, e, cnt: (b, 0)),
                  pl.BlockSpec((1, EXPERTS_PER_STEP, D_MODEL, D_EXPERT), expert),
                  pl.BlockSpec((1, EXPERTS_PER_STEP, D_MODEL, D_EXPERT), expert),
                  pl.BlockSpec((1, EXPERTS_PER_STEP, D_EXPERT, D_MODEL), expert),
                  pl.BlockSpec((1,) + sg.shape[1:], shared), pl.BlockSpec((1,) + su.shape[1:], shared),
                  pl.BlockSpec((1,) + sd.shape[1:], shared)],
        out_specs=pl.BlockSpec((tb, D_MODEL), lambda b, e, cnt: (b, 0), pipeline_mode=pl.Buffered(1)),
        scratch_shapes=[pltpu.VMEM(((tb + 1) * ROW_TILE, LANES), F32),
                        pltpu.VMEM((rm * ROW_TILE, LANES), F32),
                        pltpu.VMEM((rm * ROW_TILE, LANES), F32)],
    )
    return pl.pallas_call(
        functools.partial(_moe_routed_kernel, tb=tb, rm=rm, sub=sub, epp=EXPERTS_PER_STEP),
        grid_spec=grid_spec,
        out_shape=jax.ShapeDtypeStruct((n_blocks * tb, D_MODEL), F32),
        compiler_params=_params(2),
        name="moe_routed",
    )(offsets.reshape(-1), as_blocks(src_lists), as_blocks(gate_lists), h_tiles, wg, wu, wd, sg, su, sd)


def _moe_tiling(n_tok):
    n_blocks, n_sub = 8, 10
    tb, rm = n_tok // n_blocks, 256
    sub = tb // n_sub
    assert tb * n_blocks == n_tok and sub * n_sub == tb and sub % 16 == 0
    return tb, rm, sub


def _dispatch_lists(ids, gates, *, tb):
    assert tb <= 1 << TOKEN_BITS
    n_blocks = ids.shape[1] // tb
    per_block = lambda a: a[:TOP_K].reshape(TOP_K, n_blocks, tb).transpose(1, 0, 2).reshape(n_blocks, TOP_K * tb)
    eid, gate = per_block(ids), per_block(gates)
    tok = jnp.tile(jnp.arange(tb, dtype=jnp.int32), TOP_K)
    key, gate = lax.sort(((eid << TOKEN_BITS) | tok, gate), dimension=1, num_keys=1, is_stable=False)
    counts = jnp.sum(eid[:, None, :] == jnp.arange(N_EXPERTS, dtype=jnp.int32)[None, :, None], axis=2,
                     dtype=jnp.int32)
    offsets = jnp.concatenate([jnp.zeros((n_blocks, 1), jnp.int32), jnp.cumsum(counts, axis=1)], axis=1)
    tail = ((0, 0), (0, GATHER_UNROLL))
    return offsets, jnp.pad((key & ((1 << TOKEN_BITS) - 1)) * ROW_TILE, tail), jnp.pad(gate, tail)


def _moe_out_kernel(x_ref, y_ref, mod_ref, g_ref, b_ref, o_ref):
    o_ref[...] = _layer_norm(ALPHA * x_ref[...] + mod_ref[5] * y_ref[...], g_ref[...], b_ref[...])


def _moe_out(x, y, mods, ln_g, ln_b, *, tm, tiles_per_batch, y_first_tile):
    t_tok = x.shape[0]
    tok = lambda i: (i, 0)
    return pl.pallas_call(
        _moe_out_kernel,
        grid=(t_tok // tm,),
        in_specs=[pl.BlockSpec((tm, D_MODEL), tok),
                  pl.BlockSpec((tm, D_MODEL), lambda i: (i + y_first_tile, 0)),
                  _mod_spec(mods.shape[2], tiles_per_batch),
                  _const_spec((1, D_MODEL)), _const_spec((1, D_MODEL))],
        out_specs=pl.BlockSpec((tm, D_MODEL), tok),
        out_shape=jax.ShapeDtypeStruct((t_tok, D_MODEL), F32),
        compiler_params=_params(1),
        name="moe_out",
    )(x, y, mods, ln_g, ln_b)


def _rope_chunk(xc, cos, sin_signed, first_half):
    half = HEAD_DIM // 2
    partner = jnp.where(first_half, pltpu.roll(xc, LANES - half, 1), pltpu.roll(xc, half, 1))
    return xc * cos + partner * sin_signed


def _qkv_kernel(x_ref, mod_ref, w_ref, cos_ref, sin_ref, q_ref, k_ref, v_ref):
    h = (x_ref[...] * (1.0 + mod_ref[1]) + mod_ref[0]).astype(BF16)
    cos, sin_signed = cos_ref[...], sin_ref[...]
    lane = lax.broadcasted_iota(jnp.int32, cos.shape, 1)
    first_half = (lane % HEAD_DIM) < (HEAD_DIM // 2)
    n_chunks = D_MODEL // LANES
    for c in range(n_chunks):
        sl = slice(c * LANES, (c + 1) * LANES)
        qc = jnp.dot(h, w_ref[:, c * LANES:(c + 1) * LANES], preferred_element_type=F32)
        q_ref[:, sl] = (_rope_chunk(qc, cos, sin_signed, first_half) * (HEAD_DIM ** -0.5)).astype(BF16)
        kc = jnp.dot(h, w_ref[:, D_MODEL + c * LANES:D_MODEL + (c + 1) * LANES],
                     preferred_element_type=F32)
        k_ref[:, sl] = _rope_chunk(kc, cos, sin_signed, first_half)
    v_ref[...] = jnp.dot(h, w_ref[:, 2 * D_MODEL:], preferred_element_type=F32)


def _qkv(x, mods, w_qkv, cos, sin_signed, *, tm, tiles_per_batch, table_tiles):
    t_tok = x.shape[0]
    rows = mods.shape[2]
    tok = lambda i: (i, 0)
    return pl.pallas_call(
        _qkv_kernel,
        grid=(t_tok // tm,),
        in_specs=[pl.BlockSpec((tm, D_MODEL), tok),
                  _mod_spec(rows, tiles_per_batch),
                  _const_spec(w_qkv.shape),
                  pl.BlockSpec((tm, LANES), lambda i: (i % table_tiles, 0)),
                  pl.BlockSpec((tm, LANES), lambda i: (i % table_tiles, 0))],
        out_specs=[pl.BlockSpec((tm, D_MODEL), tok)] * 3,
        out_shape=[jax.ShapeDtypeStruct((t_tok, D_MODEL), BF16),
                   jax.ShapeDtypeStruct((t_tok, D_MODEL), F32),
                   jax.ShapeDtypeStruct((t_tok, D_MODEL), F32)],
        compiler_params=_params(1),
        name="qkv_rope",
    )(x, mods, w_qkv, cos, sin_signed)


def _qkv_prompt_kernel(x_ref, mod_ref, wq_ref, wkt_ref, wv_ref, cos_ref, sin_ref, cost_ref, sint_ref,
                       q_ref, kt_ref, v_ref, *, tm):
    h = (x_ref[...] * (1.0 + mod_ref[1]) + mod_ref[0]).astype(BF16)
    cos, sin_signed = cos_ref[...], sin_ref[...]
    lane = lax.broadcasted_iota(jnp.int32, cos.shape, 1)
    first_half = (lane % HEAD_DIM) < (HEAD_DIM // 2)
    for c in range(D_MODEL // LANES):
        sl = slice(c * LANES, (c + 1) * LANES)
        qc = jnp.dot(h, wq_ref[:, sl], preferred_element_type=F32)
        q_ref[:, sl] = (_rope_chunk(qc, cos, sin_signed, first_half) * (HEAD_DIM ** -0.5)).astype(BF16)
    kt = lax.dot_general(wkt_ref[...], h, NT_DIMS, preferred_element_type=F32)
    cos_t, sin_t = cost_ref[...], sint_ref[...]
    half = HEAD_DIM // 2
    for g in range(N_SUB):
        x1 = kt[g * HEAD_DIM:g * HEAD_DIM + half, :]
        x2 = kt[g * HEAD_DIM + half:(g + 1) * HEAD_DIM, :]
        kt_ref[0, g * HEAD_DIM:g * HEAD_DIM + half, :] = x1 * cos_t - x2 * sin_t
        kt_ref[0, g * HEAD_DIM + half:(g + 1) * HEAD_DIM, :] = x2 * cos_t + x1 * sin_t
    v = jnp.dot(h, wv_ref[...], preferred_element_type=F32)
    for hh in range(N_HEADS):
        v_ref[0, pl.ds(hh, tm, stride=N_HEADS), :] = v[:, hh * LANES:(hh + 1) * LANES]


def _qkv_prompt(x, mods, wq, wkt, wv, cos, sin_signed, cos_t, sin_t, *, n_batch, seq, tm):
    tpb = seq // tm
    tok = lambda i: (i, 0)
    return pl.pallas_call(
        functools.partial(_qkv_prompt_kernel, tm=tm),
        grid=(n_batch * tpb,),
        in_specs=[pl.BlockSpec((tm, D_MODEL), tok),
                  _mod_spec(mods.shape[2], tpb),
                  _const_spec(wq.shape), _const_spec(wkt.shape), _const_spec(wv.shape),
                  pl.BlockSpec((tm, LANES), lambda i: (i % tpb, 0)),
                  pl.BlockSpec((tm, LANES), lambda i: (i % tpb, 0)),
                  pl.BlockSpec((HEAD_DIM // 2, tm), lambda i: (0, i % tpb)),
                  pl.BlockSpec((HEAD_DIM // 2, tm), lambda i: (0, i % tpb))],
        out_specs=[pl.BlockSpec((tm, D_MODEL), tok),
                   pl.BlockSpec((1, D_MODEL, tm), lambda i: (i // tpb, 0, i % tpb)),
                   pl.BlockSpec((1, tm * N_HEADS, LANES), lambda i: (i // tpb, i % tpb, 0))],
        out_shape=[jax.ShapeDtypeStruct((n_batch * seq, D_MODEL), BF16),
                   jax.ShapeDtypeStruct((n_batch, D_MODEL, seq), F32),
                   jax.ShapeDtypeStruct((n_batch, seq * N_HEADS, LANES), F32)],
        compiler_params=_params(1),
        name="qkv_rope_prompt",
    )(x, mods, wq, wkt, wv, cos, sin_signed, cos_t, sin_t)


def _lambda_value(lq1_ref, lk1_ref, lq2_ref, lk2_ref, lam_init):
    s1 = jnp.sum(lq1_ref[...] * lk1_ref[...], axis=1, keepdims=True)
    s2 = jnp.sum(lq2_ref[...] * lk2_ref[...], axis=1, keepdims=True)
    return jnp.exp(s1) - jnp.exp(s2) + lam_init


def _diff_norm(o0, o1, lam, subln_g, lam_init):
    a = o0 - lam * o1
    a = a * lax.rsqrt(jnp.mean(a * a, axis=-1, keepdims=True) + LN_EPS) * subln_g
    return a * (1.0 - lam_init)


def _online_softmax_step(s, v, m, l, acc):
    m_new = jnp.maximum(m, jnp.max(s, axis=1, keepdims=True))
    corr = jnp.exp(m - m_new)
    p = jnp.exp(s - m_new)
    l = l * corr + jnp.sum(p, axis=1, keepdims=True)
    acc = acc * corr + jnp.dot(p.astype(BF16), v, preferred_element_type=F32)
    return m_new, l, acc


def _flash_kernel(q_ref, kt_ref, v_ref, lq1_ref, lk1_ref, lq2_ref, lk2_ref, sg_ref, o_ref,
                  kb_ref, vb_ref, *, tq, tk, n_chunks, lam_init):
    head = pl.program_id(1)
    qi = pl.program_id(2)

    @pl.when(qi == 0)
    def _():
        for c in range(n_chunks):
            for j in range(2):
                kb_ref[c, j] = kt_ref[0, j * HEAD_DIM:(j + 1) * HEAD_DIM, c * tk:(c + 1) * tk].astype(BF16)
            vb_ref[c] = v_ref[0, pl.ds(c * tk * N_HEADS + head, tk, stride=N_HEADS), :].astype(BF16)

    q_all = q_ref[...]
    first_diag = (qi * tq) // tk
    n_diag = max(tq // tk, 1)
    row = qi * tq + lax.broadcasted_iota(jnp.int32, (tq, tk), 0)
    col = first_diag * tk + lax.broadcasted_iota(jnp.int32, (tq, tk), 1)
    outs = []
    for j in range(2):
        q = q_all[:, j * HEAD_DIM:(j + 1) * HEAD_DIM]

        def body(kc, carry, q=q, j=j):
            s = jnp.dot(q, kb_ref[kc, j], preferred_element_type=F32)
            return _online_softmax_step(s, vb_ref[kc], *carry)

        init = (jnp.full((tq, 1), -jnp.inf, F32), jnp.zeros((tq, 1), F32),
                jnp.zeros((tq, 2 * HEAD_DIM), F32))
        m, l, acc = lax.fori_loop(0, first_diag, body, init)
        for d in range(n_diag):
            s = jnp.dot(q, kb_ref[first_diag + d, j], preferred_element_type=F32)
            s = jnp.where(col + d * tk <= row, s, -jnp.inf)
            m, l, acc = _online_softmax_step(s, vb_ref[first_diag + d], m, l, acc)
        outs.append(acc / l)
    lam = _lambda_value(lq1_ref, lk1_ref, lq2_ref, lk2_ref, lam_init)
    o_ref[...] = _diff_norm(outs[0], outs[1], lam, sg_ref[...], lam_init).astype(BF16)


def _flash(q, kt, v, lq1, lk1, lq2, lk2, subln_g, *, n_batch, seq, tq, tk, lam_init):
    assert (tk % tq == 0 or tq % tk == 0) and seq % tk == 0 and seq % tq == 0
    nq, nk = seq // tq, seq // tk
    lam_spec = _const_spec((1, HEAD_DIM))
    return pl.pallas_call(
        functools.partial(_flash_kernel, tq=tq, tk=tk, n_chunks=nk, lam_init=lam_init),
        grid=(n_batch, N_HEADS, nq),
        in_specs=[pl.BlockSpec((tq, LANES), lambda b, h, i: (b * nq + i, h)),
                  pl.BlockSpec((1, 2 * HEAD_DIM, seq), lambda b, h, i: (b, h, 0)),
                  pl.BlockSpec((1, seq * N_HEADS, LANES), lambda b, h, i: (b, 0, 0)),
                  lam_spec, lam_spec, lam_spec, lam_spec,
                  _const_spec((1, 2 * HEAD_DIM))],
        out_specs=pl.BlockSpec((tq, LANES), lambda b, h, i: (b * nq + i, h)),
        out_shape=jax.ShapeDtypeStruct((n_batch * seq, D_MODEL), BF16),
        scratch_shapes=[pltpu.VMEM((nk, 2, HEAD_DIM, tk), BF16), pltpu.VMEM((nk, tk, 2 * HEAD_DIM), BF16)],
        compiler_params=_params(3),
        name="prompt_attn",
    )(q, kt, v, lq1, lk1, lq2, lk2, subln_g)


def _paged_kernel(pt_ref, q_ref, kn_ref, vn_ref, lq1_ref, lk1_ref, lq2_ref, lk2_ref, sg_ref, *rest,
                  pages_per_step, n_new, lam_init):
    k_refs = rest[:pages_per_step]
    v_refs = rest[pages_per_step:2 * pages_per_step]
    o_ref, qbd_ref, m_ref, l_ref, acc_ref = rest[2 * pages_per_step:]
    p = pl.program_id(1)
    hw = 2 * HEAD_DIM
    hr = 2 * n_new

    @pl.when(p == 0)
    def _():
        q = q_ref[0]
        for h in range(N_HEADS):
            qh = q[:, h * hw:(h + 1) * hw]
            lane_j = lax.broadcasted_iota(jnp.int32, qh.shape, 1) // HEAD_DIM
            for j in range(2):
                qbd_ref[h, j * n_new:(j + 1) * n_new, :] = jnp.where(lane_j == j, qh, jnp.zeros_like(qh))
        m_ref[...] = jnp.full(m_ref.shape, -jnp.inf, F32)
        l_ref[...] = jnp.zeros(l_ref.shape, F32)
        acc_ref[...] = jnp.zeros(acc_ref.shape, F32)

    def update(s, value_of):
        m_old = m_ref[...]
        m_new = jnp.maximum(m_old, jnp.max(s, axis=1, keepdims=True))
        corr = jnp.exp(m_old - m_new)
        pr = jnp.exp(s - m_new)
        l_ref[...] = l_ref[...] * corr + jnp.sum(pr, axis=1, keepdims=True)
        pb = pr.astype(BF16)
        pvs = []
        for h in range(N_HEADS):
            pv = None
            for i in range(s.shape[1] // PAGE_SIZE):
                part = jnp.dot(pb[h * hr:(h + 1) * hr, i * PAGE_SIZE:(i + 1) * PAGE_SIZE], value_of(h, i),
                               preferred_element_type=F32)
                pv = part if pv is None else pv + part
            pvs.append(pv)
        acc_ref[...] = acc_ref[...] * corr + jnp.concatenate(pvs, axis=0)
        m_ref[...] = m_new

    s = jnp.concatenate(
        [jnp.concatenate(
            [jnp.dot(qbd_ref[h], kr[0, 0, h * hw:(h + 1) * hw, :].astype(BF16), preferred_element_type=F32)
             for kr in k_refs], axis=1) for h in range(N_HEADS)], axis=0)
    update(s, lambda h, i: v_refs[i][0, 0, pl.ds(h, PAGE_SIZE, stride=N_HEADS), :].astype(BF16))

    @pl.when(p == pl.num_programs(1) - 1)
    def _():
        s_new = jnp.concatenate(
            [lax.dot_general(qbd_ref[h], kn_ref[0, :, h * hw:(h + 1) * hw].astype(BF16), NT_DIMS,
                             preferred_element_type=F32) for h in range(N_HEADS)], axis=0)
        tok = lax.broadcasted_iota(jnp.int32, s_new.shape, 0) % n_new
        key = lax.broadcasted_iota(jnp.int32, s_new.shape, 1)
        s_new = jnp.where((key < n_new) & (key <= tok), s_new, -jnp.inf)
        update(s_new, lambda h, i: vn_ref[0, :, h * hw:(h + 1) * hw].astype(BF16))
        o = acc_ref[...] / l_ref[...]
        lam = _lambda_value(lq1_ref, lk1_ref, lq2_ref, lk2_ref, lam_init)
        for h in range(N_HEADS):
            o0 = o[h * hr:h * hr + n_new]
            o1 = o[h * hr + n_new:(h + 1) * hr]
            o_ref[0, :, h * hw:(h + 1) * hw] = _diff_norm(o0, o1, lam, sg_ref[...], lam_init).astype(BF16)


def _paged(page_table, q, k_new, v_new, cache_kt, cache_v, lq1, lk1, lq2, lk2, subln_g, *,
           layer, pages_per_step, lam_init):
    n_batch, n_pages = page_table.shape
    n_new = q.shape[1]
    steps = n_pages // pages_per_step
    lam_spec = pl.BlockSpec((1, HEAD_DIM), lambda n, p, pt: (0, 0))

    def kpage_spec(i):
        return pl.BlockSpec((1, 1, D_MODEL, PAGE_SIZE),
                            lambda n, p, pt: (layer, pt[n * n_pages + p * pages_per_step + i], 0, 0))

    def vpage_spec(i):
        return pl.BlockSpec((1, 1, PAGE_SIZE * N_HEADS, 2 * HEAD_DIM),
                            lambda n, p, pt: (layer, pt[n * n_pages + p * pages_per_step + i], 0, 0))

    per_batch = lambda n, p, pt: (n, 0, 0)
    grid_spec = pltpu.PrefetchScalarGridSpec(
        num_scalar_prefetch=1,
        grid=(n_batch, steps),
        in_specs=[pl.BlockSpec((1, n_new, D_MODEL), per_batch),
                  pl.BlockSpec((1, PAGE_SIZE, D_MODEL), per_batch),
                  pl.BlockSpec((1, PAGE_SIZE, D_MODEL), per_batch),
                  lam_spec, lam_spec, lam_spec, lam_spec,
                  pl.BlockSpec((1, 2 * HEAD_DIM), lambda n, p, pt: (0, 0))]
                 + [kpage_spec(i) for i in range(pages_per_step)]
                 + [vpage_spec(i) for i in range(pages_per_step)],
        out_specs=pl.BlockSpec((1, n_new, D_MODEL), per_batch),
        scratch_shapes=[pltpu.VMEM((N_HEADS, 2 * n_new, 2 * HEAD_DIM), BF16),
                        pltpu.VMEM((N_SUB * n_new, 1), F32),
                        pltpu.VMEM((N_SUB * n_new, 1), F32),
                        pltpu.VMEM((N_SUB * n_new, 2 * HEAD_DIM), F32)],
    )
    return pl.pallas_call(
        functools.partial(_paged_kernel, pages_per_step=pages_per_step, n_new=n_new, lam_init=lam_init),
        grid_spec=grid_spec,
        out_shape=jax.ShapeDtypeStruct((n_batch, n_new, D_MODEL), BF16),
        compiler_params=_params(2),
        name="sample_attn",
    )(page_table.reshape(-1), q, k_new, v_new, lq1, lk1, lq2, lk2, subln_g,
      *([cache_kt] * pages_per_step), *([cache_v] * pages_per_step))


def _attn_out_kernel(a_ref, x_ref, mod_ref, wo_ref, g_ref, b_ref, wr_ref, x1_ref, h1_ref, lg_ref):
    y = jnp.dot(a_ref[...], wo_ref[...], preferred_element_type=F32)
    _mixer_epilogue(x_ref[...], y, mod_ref, g_ref, b_ref, wr_ref, x1_ref, h1_ref, lg_ref)


def _attn_out(a, x, mods, w_o, ln_g, ln_b, w_router, *, tm, tiles_per_batch):
    t_tok = x.shape[0]
    rows = mods.shape[2]
    tok = lambda i: (i, 0)
    return pl.pallas_call(
        _attn_out_kernel,
        grid=(t_tok // tm,),
        in_specs=[pl.BlockSpec((tm, D_MODEL), tok), pl.BlockSpec((tm, D_MODEL), tok),
                  _mod_spec(rows, tiles_per_batch), _const_spec(w_o.shape),
                  _const_spec((1, D_MODEL)), _const_spec((1, D_MODEL)),
                  _const_spec((D_MODEL, N_EXPERTS))],
        out_specs=[pl.BlockSpec((tm, D_MODEL), tok), pl.BlockSpec((tm * ROW_TILE, LANES), tok),
                   pl.BlockSpec((tm, N_EXPERTS), tok)],
        out_shape=[jax.ShapeDtypeStruct((t_tok, D_MODEL), F32),
                   jax.ShapeDtypeStruct((t_tok * ROW_TILE, LANES), F32),
                   jax.ShapeDtypeStruct((t_tok, N_EXPERTS), F32)],
        compiler_params=_params(1),
        name="attn_out",
    )(a, x, mods, w_o, ln_g, ln_b, w_router)


def _rope_tables(pos):
    half = HEAD_DIM // 2
    inv_freq = ROPE_THETA ** (-jnp.arange(half, dtype=F32) * 2.0 / HEAD_DIM)
    ang = pos[:, None] * inv_freq[None, :]
    cos, sin = jnp.cos(ang), jnp.sin(ang)
    reps = LANES // HEAD_DIM
    cos_t = jnp.tile(jnp.concatenate([cos, cos], axis=1), (1, reps))
    sin_t = jnp.tile(jnp.concatenate([-sin, sin], axis=1), (1, reps))
    return cos_t, sin_t, cos.T, sin.T


def kernel(x_prompt, x_sample, state_pool, cache_k, cache_v, page_table, c_prompt, c_sample, w_ada, b_ada, ln1_g, ln1_b, ln2_g, ln2_b, w_pool, pool_scale, w_qkv, lambda_q1, lambda_k1, lambda_q2, lambda_k2, subln_g, w_o, w_router, router_bias, w_gate, w_up, w_down, ws_gate, ws_up, ws_down):
    n_p, l_p, d = x_prompt.shape
    n_s, l_s, _ = x_sample.shape
    t_p, t_s = n_p * l_p, n_s * l_s
    tm_p = 512
    tpb_p = l_p // tm_p
    tm_route = 256
    tb_moe, rm_moe, sub_moe = _moe_tiling(t_p + t_s)
    assert t_p % t_s == 0 and (t_p + t_s) % tm_route == 0
    row = lambda a: a.reshape(1, -1)
    experts = tuple(w.astype(BF16) for w in (w_gate, w_up, w_down, ws_gate, ws_up, ws_down))

    mods = _ada(jnp.concatenate([c_prompt, c_sample], axis=0), w_ada, b_ada)
    xp = x_prompt.reshape(t_p, d)
    xs = x_sample.reshape(t_s, d)
    cache_kt = jnp.transpose(cache_k, (0, 1, 3, 4, 5, 2)).reshape(cache_k.shape[0], cache_k.shape[1], d,
                                                                   PAGE_SIZE)
    cache_vr = cache_v.reshape(cache_v.shape[0], cache_v.shape[1], PAGE_SIZE * N_HEADS, 2 * HEAD_DIM)
    cos_p, sin_p, cos_pt, sin_pt = _rope_tables(jnp.arange(l_p, dtype=F32))
    cos_s, sin_s, _, _ = _rope_tables(jnp.arange(l_s, dtype=F32) + PAST_LEN)
    cos_s, sin_s = jnp.tile(cos_s, (n_s, 1)), jnp.tile(sin_s, (n_s, 1))

    pool_p, pool_s, kp_l, vp_l, ks_l, vs_l = [], [], [], [], [], []
    for i in range(DEPTH):
        j = i // 2
        mods_p = mods[i, :, :n_p].reshape(6, n_p, 1, d)
        mods_s_batch = mods[i, :, n_p:].reshape(6, n_s, 1, d)
        mods_s_tok = jnp.repeat(mods[i, :, n_p:], l_s, axis=1).reshape(6, 1, t_s, d)
        g1, b1, g2, b2 = row(ln1_g[i]), row(ln1_b[i]), row(ln2_g[i]), row(ln2_b[i])
        if i % 2 == 0:
            wp = w_pool[j].astype(BF16)
            ps = row(pool_scale[j])
            xp1, hp1, lgp, st_p = _pool_layer(xp, xp, mods_p, wp, ps, g1, b1, w_router[i], tm=tm_p,
                                              tiles_per_batch=tpb_p, start=0, halo_is_state=False)
            hist = jnp.pad(state_pool[j], ((0, 0), (POOL_HALO - state_pool.shape[2], 0), (0, 0)))
            xs1, hs1, lgs, st_s = _pool_layer(xs, hist, mods_s_batch, wp, ps, g1, b1, w_router[i], tm=l_s,
                                              tiles_per_batch=1, start=PAST_LEN, halo_is_state=True)
            keep = state_pool.shape[2]
            pool_p.append(st_p[:, POOL_HALO - keep:])
            pool_s.append(st_s[:, POOL_HALO - keep:])
        else:
            lam_init = 0.8 - 0.6 * math.exp(-0.3 * i)
            wq = w_qkv[j].astype(BF16)
            wo = w_o[j].astype(BF16)
            lams = (row(lambda_q1[j]), row(lambda_k1[j]), row(lambda_q2[j]), row(lambda_k2[j]))
            sg = row(subln_g[j])
            wkt = w_qkv[j][:, d:2 * d].T.astype(BF16)
            qp, kpt, vp = _qkv_prompt(xp, mods_p, wq[:, :d], wkt, wq[:, 2 * d:], cos_p, sin_p, cos_pt, sin_pt,
                                      n_batch=n_p, seq=l_p, tm=tm_p)
            ap = _flash(qp, kpt, vp, *lams, sg, n_batch=n_p, seq=l_p, tq=512, tk=512, lam_init=lam_init)
            xp1, hp1, lgp = _attn_out(ap, xp, mods_p, wo, g1, b1, w_router[i], tm=tm_p,
                                      tiles_per_batch=tpb_p)
            qs, ks, vs = _qkv(xs, mods_s_tok, wq, cos_s, sin_s, tm=t_s, tiles_per_batch=1, table_tiles=1)
            pad_new = lambda a: jnp.pad(a.reshape(n_s, l_s, d), ((0, 0), (0, PAGE_SIZE - l_s), (0, 0)))
            a_s = _paged(page_table, qs.reshape(n_s, l_s, d), pad_new(ks), pad_new(vs), cache_kt, cache_vr,
                         *lams, sg, layer=j, pages_per_step=16, lam_init=lam_init)
            xs1, hs1, lgs = _attn_out(a_s.reshape(t_s, d), xs, mods_s_tok, wo, g1, b1, w_router[i],
                                      tm=t_s, tiles_per_batch=1)
            kp_l.append(kpt.reshape(n_p, N_HEADS, 2, HEAD_DIM, l_p).transpose(0, 4, 1, 2, 3))
            vp_l.append(vp.reshape(n_p, l_p, N_HEADS, 2 * HEAD_DIM))
            ks_l.append(ks.reshape(n_s, l_s, N_HEADS, 2, HEAD_DIM))
            vs_l.append(vs.reshape(n_s, l_s, N_HEADS, 2 * HEAD_DIM))
        picks = _router(jnp.concatenate([lgp, lgs], axis=0), router_bias[i].reshape(N_EXPERTS, 1), tm=tm_route)
        lists = _dispatch_lists(*picks, tb=tb_moe)
        y = _moe_routed(*lists, jnp.concatenate([hp1, hs1], axis=0), *experts, layer=i, tb=tb_moe, rm=rm_moe,
                        sub=sub_moe)
        xp = _moe_out(xp1, y, mods_p, g2, b2, tm=tm_p, tiles_per_batch=tpb_p, y_first_tile=0)
        xs = _moe_out(xs1, y, mods_s_tok, g2, b2, tm=t_s, tiles_per_batch=1, y_first_tile=t_p // t_s)
    return (xp.reshape(n_p, l_p, d), xs.reshape(n_s, l_s, d), jnp.stack(pool_p), jnp.stack(pool_s),
            jnp.stack(kp_l), jnp.stack(vp_l), jnp.stack(ks_l), jnp.stack(vs_l))
```

```python
import functools
import math

import jax
import jax.numpy as jnp
from jax import lax
from jax.experimental import pallas as pl
from jax.experimental.pallas import tpu as pltpu

F32 = jnp.float32
BF16 = jnp.bfloat16
HIGHEST = lax.Precision.HIGHEST

D_MODEL = 1024
DEPTH = 2
PAST_LEN = 16384
PAGE_SIZE = 128
POOL_WINDOWS = (2, 4, 8, 16)
POOL_CH = D_MODEL // len(POOL_WINDOWS)
POOL_HALO = 16
HEAD_DIM = 64
N_HEADS = D_MODEL // (2 * HEAD_DIM)
N_SUB = 2 * N_HEADS
ROPE_THETA = 10000.0
N_EXPERTS = 64
TOP_K = 6
N_GROUPS = 8
GROUP_SIZE = N_EXPERTS // N_GROUPS
TOPK_GROUPS = 4
D_EXPERT = 256
ROUTED_SCALE = 2.5
ALPHA = (2 * DEPTH) ** 0.25
LN_EPS = 1e-5
LANES = 128
ROW_TILE = D_MODEL // LANES
GATHER_UNROLL = 16
EXPERTS_PER_STEP = 2
PICK_ROWS = 8
TOKEN_BITS = 12
VMEM_LIMIT = 56 * 1024 * 1024

NT_DIMS = (((1,), (1,)), ((), ()))


def _params(n_axes):
    return pltpu.CompilerParams(dimension_semantics=("arbitrary",) * n_axes,
                                vmem_limit_bytes=VMEM_LIMIT)


def _layer_norm(x, g, b):
    mu = jnp.mean(x, axis=-1, keepdims=True)
    xc = x - mu
    var = jnp.mean(xc * xc, axis=-1, keepdims=True)
    return xc * lax.rsqrt(var + LN_EPS) * g + b


def _mod_spec(rows, tiles_per_batch):
    return pl.BlockSpec((6, None, rows, D_MODEL), lambda i: (0, i // tiles_per_batch, 0, 0))


def _const_spec(shape):
    nd = len(shape)
    return pl.BlockSpec(shape, lambda *_: (0,) * nd)


def _ada_kernel(c_ref, w_ref, b_ref, o_ref):
    c = c_ref[...]
    s = c * jax.nn.sigmoid(c)
    o_ref[0, 0] = jnp.dot(s, w_ref[0], preferred_element_type=F32, precision=HIGHEST) + b_ref[0, 0]


def _ada(c_all, w_ada, b_ada):
    n = c_all.shape[0]
    depth = w_ada.shape[0]
    return pl.pallas_call(
        _ada_kernel,
        grid=(depth, 6),
        in_specs=[pl.BlockSpec((n, D_MODEL), lambda i, j: (0, 0)),
                  pl.BlockSpec((1, D_MODEL, D_MODEL), lambda i, j: (i, 0, j)),
                  pl.BlockSpec((1, 1, 1, D_MODEL), lambda i, j: (i, j, 0, 0))],
        out_specs=pl.BlockSpec((1, 1, n, D_MODEL), lambda i, j: (i, j, 0, 0)),
        out_shape=jax.ShapeDtypeStruct((depth, 6, n, D_MODEL), F32),
        compiler_params=_params(2),
        name="ada",
    )(c_all, w_ada, b_ada.reshape(depth, 6, 1, D_MODEL))


def _tiles_to_rows(ref, first_tok, n_tok):
    return jnp.concatenate(
        [ref[pl.ds(first_tok * ROW_TILE + c, n_tok, stride=ROW_TILE), :] for c in range(ROW_TILE)], axis=1)


def _rows_to_tiles(ref, first_tok, val):
    for c in range(ROW_TILE):
        ref[pl.ds(first_tok * ROW_TILE + c, val.shape[0], stride=ROW_TILE), :] = val[:, c * LANES:(c + 1) * LANES]


def _moe_input_outputs(n_all, tm, first_tile):
    at = lambda i: (i + first_tile, 0)
    return ([pl.BlockSpec((tm * ROW_TILE, LANES), at), pl.BlockSpec((tm, N_EXPERTS), at)],
            [jax.ShapeDtypeStruct((n_all * ROW_TILE, LANES), F32), jax.ShapeDtypeStruct((n_all, N_EXPERTS), F32)])


def _skip_merged_refs(body, n_in, merged):
    if not merged:
        return body

    def with_merged(*refs):
        return body(*refs[:n_in], *refs[n_in + len(merged):])

    return with_merged


def _mixer_epilogue(x, y, mod_ref, g_ref, b_ref, wr_ref, x1_ref, h1_ref, lg_ref):
    x1 = _layer_norm(ALPHA * x + mod_ref[2] * y, g_ref[...], b_ref[...])
    h1 = x1 * (1.0 + mod_ref[4]) + mod_ref[3]
    x1_ref[...] = x1
    _rows_to_tiles(h1_ref, 0, h1)
    lg_ref[...] = jnp.dot(h1, wr_ref[...], preferred_element_type=F32, precision=HIGHEST)


def _pool_kernel(x_ref, halo_ref, mod_ref, wp_ref, ps_ref, g_ref, b_ref, wr_ref,
                 x1_ref, h1_ref, lg_ref, st_ref, *, tm, tiles_per_batch, start, halo_is_state):
    t = pl.program_id(0) % tiles_per_batch
    shift, scale = mod_ref[0], mod_ref[1]
    x = x_ref[...]
    h = x * (1.0 + scale) + shift
    if halo_is_state:
        halo = halo_ref[...]
    else:
        halo = halo_ref[...] * (1.0 + scale) + shift
        halo = jnp.where(t == 0, 0.0, halo)
    ext = jnp.concatenate([halo, h], axis=0)
    c = POOL_CH
    s2 = ext + pltpu.roll(ext, 1, 0)
    s4 = s2[:, c:] + pltpu.roll(s2[:, c:], 2, 0)
    s8 = s4[:, c:] + pltpu.roll(s4[:, c:], 4, 0)
    s16 = s8[:, c:] + pltpu.roll(s8[:, c:], 8, 0)
    wins = (s2[POOL_HALO:, :c], s4[POOL_HALO:, :c], s8[POOL_HALO:, :c], s16[POOL_HALO:, :])
    pos = start + t * tm + lax.broadcasted_iota(jnp.int32, (tm, c), 0)
    ys = []
    for g, w in enumerate(POOL_WINDOWS):
        cnt = jnp.minimum(pos + 1, w).astype(F32)
        pooled = wins[g] / cnt - h[:, g * c:(g + 1) * c]
        ys.append(jnp.dot(pooled.astype(BF16), wp_ref[g], preferred_element_type=F32))
    y = jnp.concatenate(ys, axis=1) * ps_ref[...]
    st_ref[...] = ext[tm:, :]
    _mixer_epilogue(x, y, mod_ref, g_ref, b_ref, wr_ref, x1_ref, h1_ref, lg_ref)


def _pool_layer(x, halo, mods, w_pool, pool_scale, ln_g, ln_b, w_router, *, tm, tiles_per_batch,
                start, halo_is_state, n_all, first_tile, merged=()):
    t_tok = x.shape[0]
    n_batch = t_tok // (tm * tiles_per_batch)
    rows = mods.shape[2]
    moe_specs, moe_shapes = _moe_input_outputs(n_all, tm, first_tile)
    n_in = 8
    if halo_is_state:
        halo_spec = pl.BlockSpec((None, POOL_HALO, D_MODEL), lambda i: (i, 0, 0))
    else:
        halo_spec = pl.BlockSpec((POOL_HALO, D_MODEL),
                                 lambda i: (jnp.maximum(i * (tm // POOL_HALO) - 1, 0), 0))
    kern = functools.partial(_pool_kernel, tm=tm, tiles_per_batch=tiles_per_batch, start=start,
                             halo_is_state=halo_is_state)
    return pl.pallas_call(
        _skip_merged_refs(kern, n_in, merged),
        grid=(t_tok // tm,),
        in_specs=[pl.BlockSpec((tm, D_MODEL), lambda i: (i, 0)),
                  halo_spec,
                  _mod_spec(rows, tiles_per_batch),
                  _const_spec(w_pool.shape),
                  _const_spec((1, D_MODEL)), _const_spec((1, D_MODEL)), _const_spec((1, D_MODEL)),
                  _const_spec((D_MODEL, N_EXPERTS))] + [pl.BlockSpec(memory_space=pl.ANY)] * len(merged),
        out_specs=[pl.BlockSpec((tm, D_MODEL), lambda i: (i, 0))] + moe_specs
                  + [pl.BlockSpec((None, POOL_HALO, D_MODEL), lambda i: (i // tiles_per_batch, 0, 0))],
        out_shape=[jax.ShapeDtypeStruct((t_tok, D_MODEL), F32)] + moe_shapes
                  + [jax.ShapeDtypeStruct((n_batch, POOL_HALO, D_MODEL), F32)],
        input_output_aliases={n_in + k: 1 + k for k in range(len(merged))},
        compiler_params=_params(1),
        name="pool_mixer",
    )(x, halo, mods, w_pool, pool_scale, ln_g, ln_b, w_router, *merged)


def _first_max_onehot(vals, idx):
    m = vals[0]
    for v in vals[1:]:
        m = jnp.maximum(m, v)
    m = jnp.max(m, axis=0, keepdims=True)
    first = None
    for v, ix in zip(vals, idx):
        cand = jnp.min(jnp.where(v == m, ix, N_EXPERTS), axis=0, keepdims=True)
        first = cand if first is None else jnp.minimum(first, cand)
    return [ix == first for ix in idx], first


def _router_kernel(lg_ref, bias_ref, ids_ref, gates_ref, *, tm):
    lg = lg_ref[...]
    lgt = jnp.concatenate([lg, jnp.zeros((tm, LANES - N_EXPERTS), F32)], axis=1).T
    bias = bias_ref[...]
    row = lax.broadcasted_iota(jnp.int32, (GROUP_SIZE, tm), 0)
    scores, choice, idx = [], [], []
    for g in range(N_GROUPS):
        sl = slice(g * GROUP_SIZE, (g + 1) * GROUP_SIZE)
        s = jax.nn.sigmoid(lgt[sl, :])
        scores.append(s)
        choice.append(s + bias[sl, :])
        idx.append(row + g * GROUP_SIZE)
    gscore = []
    for g in range(N_GROUPS):
        ch = choice[g]
        m1 = jnp.max(ch, axis=0, keepdims=True)
        first = jnp.min(jnp.where(ch == m1, row, GROUP_SIZE), axis=0, keepdims=True)
        m2 = jnp.max(jnp.where(row == first, -jnp.inf, ch), axis=0, keepdims=True)
        gscore.append(m1 + m2)
    masked = []
    for g in range(N_GROUPS):
        rank = jnp.zeros((1, tm), jnp.int32)
        for o in range(N_GROUPS):
            if o == g:
                continue
            beats = (gscore[o] > gscore[g]) | ((gscore[o] == gscore[g]) & (o < g))
            rank = rank + beats.astype(jnp.int32)
        masked.append(jnp.where(rank < TOPK_GROUPS, choice[g], -jnp.inf))
    ids, picked = [], []
    for _ in range(TOP_K):
        hit, first = _first_max_onehot(masked, idx)
        masked = [jnp.where(h, -jnp.inf, v) for v, h in zip(masked, hit)]
        score = jnp.where(hit[0], scores[0], 0.0)
        for h, sc in zip(hit[1:], scores[1:]):
            score = score + jnp.where(h, sc, 0.0)
        ids.append(first)
        picked.append(jnp.sum(score, axis=0, keepdims=True))
    tot = picked[0]
    for p in picked[1:]:
        tot = tot + p
    fill = PICK_ROWS - TOP_K
    ids_ref[...] = jnp.concatenate(ids + [jnp.zeros((fill, tm), jnp.int32)], axis=0)
    gates_ref[...] = jnp.concatenate([p / tot * ROUTED_SCALE for p in picked] + [jnp.zeros((fill, tm), F32)],
                                     axis=0)


def _router(logits, bias, *, tm):
    t_tok = logits.shape[0]
    pick_spec = pl.BlockSpec((PICK_ROWS, tm), lambda i: (0, i))
    return pl.pallas_call(
        functools.partial(_router_kernel, tm=tm),
        grid=(t_tok // tm,),
        in_specs=[pl.BlockSpec((tm, N_EXPERTS), lambda i: (i, 0)),
                  _const_spec((N_EXPERTS, 1))],
        out_specs=[pick_spec, pick_spec],
        out_shape=[jax.ShapeDtypeStruct((PICK_ROWS, t_tok), jnp.int32),
                   jax.ShapeDtypeStruct((PICK_ROWS, t_tok), F32)],
        compiler_params=_params(1),
        name="router",
    )(logits, bias)


def _swiglu_hidden(h, wg, wu):
    a = jnp.dot(h, wg, preferred_element_type=F32)
    u = jnp.dot(h, wu, preferred_element_type=F32)
    return a * jax.nn.sigmoid(a) * u


def _moe_routed_kernel(off_ref, src_ref, gl_ref, h_ref, wg_ref, wu_ref, wd_ref, sg_ref, su_ref, sd_ref,
                       o_ref, acc_ref, xg_ref, yb_ref, *, tb, rm, sub, epp):
    b = pl.program_id(0)
    step = pl.program_id(1)
    tile = lambda i: pl.ds(pl.multiple_of(i * ROW_TILE, ROW_TILE), ROW_TILE)
    rows_at = lambda off: pl.ds(pl.multiple_of(off, ROW_TILE), ROW_TILE)

    @pl.when(step == 0)
    def _():
        @pl.when(b == 0)
        def _():
            xg_ref[...] = jnp.zeros(xg_ref.shape, F32)

        sg, su, sd = sg_ref[0], su_ref[0], sd_ref[0]

        def shared(s, c):
            x = _tiles_to_rows(h_ref, s * sub, sub).astype(BF16)
            y = jnp.dot(_swiglu_hidden(x, sg, su).astype(BF16), sd, preferred_element_type=F32)
            _rows_to_tiles(acc_ref, s * sub, y)
            return c

        lax.fori_loop(0, tb // sub, shared, 0)
        acc_ref[tile(tb)] = jnp.zeros((ROW_TILE, LANES), F32)

    def route(k):
        e = step * epp + k
        first = off_ref[b * (N_EXPERTS + 1) + e]
        n = off_ref[b * (N_EXPERTS + 1) + e + 1] - first

        def chunk(j, carry):
            base = first + j * rm
            m = jnp.minimum(n - j * rm, rm)
            groups = pl.cdiv(m, GATHER_UNROLL)

            def gather(i, c):
                for u in range(GATHER_UNROLL):
                    r = i * GATHER_UNROLL + u
                    xg_ref[tile(r)] = h_ref[rows_at(src_ref[0, 0, base + r])]
                return c

            lax.fori_loop(0, groups, gather, 0)
            x = _tiles_to_rows(xg_ref, 0, rm).astype(BF16)
            y = jnp.dot(_swiglu_hidden(x, wg_ref[0, k], wu_ref[0, k]).astype(BF16), wd_ref[0, k],
                        preferred_element_type=F32)
            _rows_to_tiles(yb_ref, 0, y)

            def scatter_group(i, partial):
                rows = []
                for u in range(GATHER_UNROLL):
                    r = i * GATHER_UNROLL + u
                    t, g = src_ref[0, 0, base + r], gl_ref[0, 0, base + r]
                    if partial:
                        t, g = jnp.where(r < m, t, tb * ROW_TILE), jnp.where(r < m, g, 0.0)
                    rows.append((t, g, r))
                vals = [acc_ref[rows_at(t)] + g * yb_ref[tile(r)] for t, g, r in rows]
                for (t, g, r), v in zip(rows, vals):
                    acc_ref[rows_at(t)] = v

            def scatter_full(i, c):
                scatter_group(i, False)
                return c

            lax.fori_loop(0, m // GATHER_UNROLL, scatter_full, 0)

            @pl.when(m % GATHER_UNROLL != 0)
            def _():
                scatter_group(m // GATHER_UNROLL, True)

            return carry

        lax.fori_loop(0, pl.cdiv(n, rm), chunk, 0)

    for k in range(epp):
        route(k)

    @pl.when(step == pl.num_programs(1) - 1)
    def _():
        def emit(s, c):
            o_ref[pl.ds(pl.multiple_of(s * sub, 8), sub), :] = _tiles_to_rows(acc_ref, s * sub, sub)
            return c

        lax.fori_loop(0, tb // sub, emit, 0)


def _moe_routed(offsets, src_lists, gate_lists, h_tiles, wg, wu, wd, sg, su, sd, *, layer, tb, rm, sub):
    n_blocks, n_list = src_lists.shape
    as_blocks = lambda a: a.reshape(n_blocks, 1, n_list)
    expert = lambda b, e, off: (layer, e, 0, 0)
    shared = lambda b, e, off: (layer, 0, 0)
    list_spec = pl.BlockSpec((1, 1, n_list), lambda b, e, off: (b, 0, 0), memory_space=pltpu.SMEM)
    grid_spec = pltpu.PrefetchScalarGridSpec(
        num_scalar_prefetch=1,
        grid=(n_blocks, N_EXPERTS // EXPERTS_PER_STEP),
        in_specs=[list_spec, list_spec,
                  pl.BlockSpec((tb * ROW_TILE, LANES), lambda b, e, cnt: (b, 0)),
                  pl.BlockSpec((1, EXPERTS_PER_STEP, D_MODEL, D_EXPERT), expert),
                  pl.BlockSpec((1, EXPERTS_PER_STEP, D_MODEL, D_EXPERT), expert),
                  pl.BlockSpec((1, EXPERTS_PER_STEP, D_EXPERT, D_MODEL), expert),
                  pl.BlockSpec((1,) + sg.shape[1:], shared), pl.BlockSpec((1,) + su.shape[1:], shared),
                  pl.BlockSpec((1,) + sd.shape[1:], shared)],
        out_specs=pl.BlockSpec((tb, D_MODEL), lambda b, e, cnt: (b, 0), pipeline_mode=pl.Buffered(1)),
        scratch_shapes=[pltpu.VMEM(((tb + 1) * ROW_TILE, LANES), F32),
                        pltpu.VMEM((rm * ROW_TILE, LANES), F32),
                        pltpu.VMEM((rm * ROW_TILE, LANES), F32)],
    )
    return pl.pallas_call(
        functools.partial(_moe_routed_kernel, tb=tb, rm=rm, sub=sub, epp=EXPERTS_PER_STEP),
        grid_spec=grid_spec,
        out_shape=jax.ShapeDtypeStruct((n_blocks * tb, D_MODEL), F32),
        compiler_params=_params(2),
        name="moe_routed",
    )(offsets.reshape(-1), as_blocks(src_lists), as_blocks(gate_lists), h_tiles, wg, wu, wd, sg, su, sd)


def _moe_tiling(n_tok):
    n_blocks, n_sub = 8, 10
    tb, rm = n_tok // n_blocks, 256
    sub = tb // n_sub
    assert tb * n_blocks == n_tok and sub * n_sub == tb and sub % 16 == 0
    return tb, rm, sub


def _dispatch_lists(ids, gates, *, tb):
    assert tb <= 1 << TOKEN_BITS
    n_blocks = ids.shape[1] // tb
    per_block = lambda a: a[:TOP_K].reshape(TOP_K, n_blocks, tb).transpose(1, 0, 2).reshape(n_blocks, TOP_K * tb)
    eid, gate = per_block(ids), per_block(gates)
    tok = jnp.tile(jnp.arange(tb, dtype=jnp.int32), TOP_K)
    key, gate = lax.sort(((eid << TOKEN_BITS) | tok, gate), dimension=1, num_keys=1, is_stable=False)
    counts = jnp.sum(eid[:, None, :] == jnp.arange(N_EXPERTS, dtype=jnp.int32)[None, :, None], axis=2,
                     dtype=jnp.int32)
    offsets = jnp.concatenate([jnp.zeros((n_blocks, 1), jnp.int32), jnp.cumsum(counts, axis=1)], axis=1)
    tail = ((0, 0), (0, GATHER_UNROLL))
    return offsets, jnp.pad((key & ((1 << TOKEN_BITS) - 1)) * ROW_TILE, tail), jnp.pad(gate, tail)


def _moe_out_kernel(x_ref, y_ref, mod_ref, g_ref, b_ref, o_ref):
    o_ref[...] = _layer_norm(ALPHA * x_ref[...] + mod_ref[5] * y_ref[...], g_ref[...], b_ref[...])


def _moe_out(x, y, mods, ln_g, ln_b, *, tm, tiles_per_batch, y_first_tile):
    t_tok = x.shape[0]
    tok = lambda i: (i, 0)
    return pl.pallas_call(
        _moe_out_kernel,
        grid=(t_tok // tm,),
        in_specs=[pl.BlockSpec((tm, D_MODEL), tok),
                  pl.BlockSpec((tm, D_MODEL), lambda i: (i + y_first_tile, 0)),
                  _mod_spec(mods.shape[2], tiles_per_batch),
                  _const_spec((1, D_MODEL)), _const_spec((1, D_MODEL))],
        out_specs=pl.BlockSpec((tm, D_MODEL), tok),
        out_shape=jax.ShapeDtypeStruct((t_tok, D_MODEL), F32),
        compiler_params=_params(1),
        name="moe_out",
    )(x, y, mods, ln_g, ln_b)


def _rope_chunk(xc, cos, sin_signed, first_half):
    half = HEAD_DIM // 2
    partner = jnp.where(first_half, pltpu.roll(xc, LANES - half, 1), pltpu.roll(xc, half, 1))
    return xc * cos + partner * sin_signed


def _qkv_kernel(x_ref, mod_ref, w_ref, cos_ref, sin_ref, q_ref, k_ref, v_ref):
    h = (x_ref[...] * (1.0 + mod_ref[1]) + mod_ref[0]).astype(BF16)
    cos, sin_signed = cos_ref[...], sin_ref[...]
    lane = lax.broadcasted_iota(jnp.int32, cos.shape, 1)
    first_half = (lane % HEAD_DIM) < (HEAD_DIM // 2)
    n_chunks = D_MODEL // LANES
    for c in range(n_chunks):
        sl = slice(c * LANES, (c + 1) * LANES)
        qc = jnp.dot(h, w_ref[:, c * LANES:(c + 1) * LANES], preferred_element_type=F32)
        q_ref[:, sl] = (_rope_chunk(qc, cos, sin_signed, first_half) * (HEAD_DIM ** -0.5)).astype(BF16)
        kc = jnp.dot(h, w_ref[:, D_MODEL + c * LANES:D_MODEL + (c + 1) * LANES],
                     preferred_element_type=F32)
        k_ref[:, sl] = _rope_chunk(kc, cos, sin_signed, first_half)
    v_ref[...] = jnp.dot(h, w_ref[:, 2 * D_MODEL:], preferred_element_type=F32)


def _qkv(x, mods, w_qkv, cos, sin_signed, *, tm, tiles_per_batch, table_tiles):
    t_tok = x.shape[0]
    rows = mods.shape[2]
    tok = lambda i: (i, 0)
    return pl.pallas_call(
        _qkv_kernel,
        grid=(t_tok // tm,),
        in_specs=[pl.BlockSpec((tm, D_MODEL), tok),
                  _mod_spec(rows, tiles_per_batch),
                  _const_spec(w_qkv.shape),
                  pl.BlockSpec((tm, LANES), lambda i: (i % table_tiles, 0)),
                  pl.BlockSpec((tm, LANES), lambda i: (i % table_tiles, 0))],
        out_specs=[pl.BlockSpec((tm, D_MODEL), tok)] * 3,
        out_shape=[jax.ShapeDtypeStruct((t_tok, D_MODEL), BF16),
                   jax.ShapeDtypeStruct((t_tok, D_MODEL), F32),
                   jax.ShapeDtypeStruct((t_tok, D_MODEL), F32)],
        compiler_params=_params(1),
        name="qkv_rope",
    )(x, mods, w_qkv, cos, sin_signed)


def _qkv_prompt_kernel(x_ref, mod_ref, wq_ref, wkt_ref, wv_ref, cos_ref, sin_ref, cost_ref, sint_ref,
                       q_ref, kt_ref, v_ref, *, tm):
    h = (x_ref[...] * (1.0 + mod_ref[1]) + mod_ref[0]).astype(BF16)
    cos, sin_signed = cos_ref[...], sin_ref[...]
    lane = lax.broadcasted_iota(jnp.int32, cos.shape, 1)
    first_half = (lane % HEAD_DIM) < (HEAD_DIM // 2)
    for c in range(D_MODEL // LANES):
        sl = slice(c * LANES, (c + 1) * LANES)
        qc = jnp.dot(h, wq_ref[:, sl], preferred_element_type=F32)
        q_ref[:, sl] = (_rope_chunk(qc, cos, sin_signed, first_half) * (HEAD_DIM ** -0.5)).astype(BF16)
    kt = lax.dot_general(wkt_ref[...], h, NT_DIMS, preferred_element_type=F32)
    cos_t, sin_t = cost_ref[...], sint_ref[...]
    half = HEAD_DIM // 2
    for g in range(N_SUB):
        x1 = kt[g * HEAD_DIM:g * HEAD_DIM + half, :]
        x2 = kt[g * HEAD_DIM + half:(g + 1) * HEAD_DIM, :]
        kt_ref[0, g * HEAD_DIM:g * HEAD_DIM + half, :] = x1 * cos_t - x2 * sin_t
        kt_ref[0, g * HEAD_DIM + half:(g + 1) * HEAD_DIM, :] = x2 * cos_t + x1 * sin_t
    v = jnp.dot(h, wv_ref[...], preferred_element_type=F32)
    for hh in range(N_HEADS):
        v_ref[0, pl.ds(hh, tm, stride=N_HEADS), :] = v[:, hh * LANES:(hh + 1) * LANES]


def _qkv_prompt(x, mods, wq, wkt, wv, cos, sin_signed, cos_t, sin_t, *, n_batch, seq, tm):
    tpb = seq // tm
    tok = lambda i: (i, 0)
    return pl.pallas_call(
        functools.partial(_qkv_prompt_kernel, tm=tm),
        grid=(n_batch * tpb,),
        in_specs=[pl.BlockSpec((tm, D_MODEL), tok),
                  _mod_spec(mods.shape[2], tpb),
                  _const_spec(wq.shape), _const_spec(wkt.shape), _const_spec(wv.shape),
                  pl.BlockSpec((tm, LANES), lambda i: (i % tpb, 0)),
                  pl.BlockSpec((tm, LANES), lambda i: (i % tpb, 0)),
                  pl.BlockSpec((HEAD_DIM // 2, tm), lambda i: (0, i % tpb)),
                  pl.BlockSpec((HEAD_DIM // 2, tm), lambda i: (0, i % tpb))],
        out_specs=[pl.BlockSpec((tm, D_MODEL), tok),
                   pl.BlockSpec((1, D_MODEL, tm), lambda i: (i // tpb, 0, i % tpb)),
                   pl.BlockSpec((1, tm * N_HEADS, LANES), lambda i: (i // tpb, i % tpb, 0))],
        out_shape=[jax.ShapeDtypeStruct((n_batch * seq, D_MODEL), BF16),
                   jax.ShapeDtypeStruct((n_batch, D_MODEL, seq), F32),
                   jax.ShapeDtypeStruct((n_batch, seq * N_HEADS, LANES), F32)],
        compiler_params=_params(1),
        name="qkv_rope_prompt",
    )(x, mods, wq, wkt, wv, cos, sin_signed, cos_t, sin_t)


def _lambda_value(lq1_ref, lk1_ref, lq2_ref, lk2_ref, lam_init):
    s1 = jnp.sum(lq1_ref[...] * lk1_ref[...], axis=1, keepdims=True)
    s2 = jnp.sum(lq2_ref[...] * lk2_ref[...], axis=1, keepdims=True)
    return jnp.exp(s1) - jnp.exp(s2) + lam_init


def _diff_norm(o0, o1, lam, subln_g, lam_init):
    a = o0 - lam * o1
    a = a * lax.rsqrt(jnp.mean(a * a, axis=-1, keepdims=True) + LN_EPS) * subln_g
    return a * (1.0 - lam_init)


def _online_softmax_step(s, v, m, l, acc):
    m_new = jnp.maximum(m, jnp.max(s, axis=1, keepdims=True))
    corr = jnp.exp(m - m_new)
    p = jnp.exp(s - m_new)
    l = l * corr + jnp.sum(p, axis=1, keepdims=True)
    acc = acc * corr + jnp.dot(p.astype(BF16), v, preferred_element_type=F32)
    return m_new, l, acc


def _flash_kernel(q_ref, kt_ref, v_ref, lq1_ref, lk1_ref, lq2_ref, lk2_ref, sg_ref, o_ref,
                  kb_ref, vb_ref, *, tq, tk, n_chunks, lam_init):
    head = pl.program_id(1)
    qi = pl.program_id(2)

    @pl.when(qi == 0)
    def _():
        for c in range(n_chunks):
            for j in range(2):
                kb_ref[c, j] = kt_ref[0, j * HEAD_DIM:(j + 1) * HEAD_DIM, c * tk:(c + 1) * tk].astype(BF16)
            vb_ref[c] = v_ref[0, pl.ds(c * tk * N_HEADS + head, tk, stride=N_HEADS), :].astype(BF16)

    q_all = q_ref[...]
    first_diag = (qi * tq) // tk
    n_diag = max(tq // tk, 1)
    row = qi * tq + lax.broadcasted_iota(jnp.int32, (tq, tk), 0)
    col = first_diag * tk + lax.broadcasted_iota(jnp.int32, (tq, tk), 1)
    outs = []
    for j in range(2):
        q = q_all[:, j * HEAD_DIM:(j + 1) * HEAD_DIM]

        def body(kc, carry, q=q, j=j):
            s = jnp.dot(q, kb_ref[kc, j], preferred_element_type=F32)
            return _online_softmax_step(s, vb_ref[kc], *carry)

        init = (jnp.full((tq, 1), -jnp.inf, F32), jnp.zeros((tq, 1), F32),
                jnp.zeros((tq, 2 * HEAD_DIM), F32))
        m, l, acc = lax.fori_loop(0, first_diag, body, init)
        for d in range(n_diag):
            s = jnp.dot(q, kb_ref[first_diag + d, j], preferred_element_type=F32)
            s = jnp.where(col + d * tk <= row, s, -jnp.inf)
            m, l, acc = _online_softmax_step(s, vb_ref[first_diag + d], m, l, acc)
        outs.append(acc / l)
    lam = _lambda_value(lq1_ref, lk1_ref, lq2_ref, lk2_ref, lam_init)
    o_ref[...] = _diff_norm(outs[0], outs[1], lam, sg_ref[...], lam_init).astype(BF16)


def _flash(q, kt, v, lq1, lk1, lq2, lk2, subln_g, *, n_batch, seq, tq, tk, lam_init):
    assert (tk % tq == 0 or tq % tk == 0) and seq % tk == 0 and seq % tq == 0
    nq, nk = seq // tq, seq // tk
    lam_spec = _const_spec((1, HEAD_DIM))
    return pl.pallas_call(
        functools.partial(_flash_kernel, tq=tq, tk=tk, n_chunks=nk, lam_init=lam_init),
        grid=(n_batch, N_HEADS, nq),
        in_specs=[pl.BlockSpec((tq, LANES), lambda b, h, i: (b * nq + i, h)),
                  pl.BlockSpec((1, 2 * HEAD_DIM, seq), lambda b, h, i: (b, h, 0)),
                  pl.BlockSpec((1, seq * N_HEADS, LANES), lambda b, h, i: (b, 0, 0)),
                  lam_spec, lam_spec, lam_spec, lam_spec,
                  _const_spec((1, 2 * HEAD_DIM))],
        out_specs=pl.BlockSpec((tq, LANES), lambda b, h, i: (b * nq + i, h)),
        out_shape=jax.ShapeDtypeStruct((n_batch * seq, D_MODEL), BF16),
        scratch_shapes=[pltpu.VMEM((nk, 2, HEAD_DIM, tk), BF16), pltpu.VMEM((nk, tk, 2 * HEAD_DIM), BF16)],
        compiler_params=_params(3),
        name="prompt_attn",
    )(q, kt, v, lq1, lk1, lq2, lk2, subln_g)


def _paged_kernel(pt_ref, q_ref, kn_ref, vn_ref, lq1_ref, lk1_ref, lq2_ref, lk2_ref, sg_ref, *rest,
                  pages_per_step, n_new, lam_init):
    k_refs = rest[:pages_per_step]
    v_refs = rest[pages_per_step:2 * pages_per_step]
    o_ref, qbd_ref, m_ref, l_ref, acc_ref = rest[2 * pages_per_step:]
    p = pl.program_id(1)
    hw = 2 * HEAD_DIM
    hr = 2 * n_new

    @pl.when(p == 0)
    def _():
        q = q_ref[0]
        for h in range(N_HEADS):
            qh = q[:, h * hw:(h + 1) * hw]
            lane_j = lax.broadcasted_iota(jnp.int32, qh.shape, 1) // HEAD_DIM
            for j in range(2):
                qbd_ref[h, j * n_new:(j + 1) * n_new, :] = jnp.where(lane_j == j, qh, jnp.zeros_like(qh))
        m_ref[...] = jnp.full(m_ref.shape, -jnp.inf, F32)
        l_ref[...] = jnp.zeros(l_ref.shape, F32)
        acc_ref[...] = jnp.zeros(acc_ref.shape, F32)

    def update(s, value_of):
        m_old = m_ref[...]
        m_new = jnp.maximum(m_old, jnp.max(s, axis=1, keepdims=True))
        corr = jnp.exp(m_old - m_new)
        pr = jnp.exp(s - m_new)
        l_ref[...] = l_ref[...] * corr + jnp.sum(pr, axis=1, keepdims=True)
        pb = pr.astype(BF16)
        pvs = []
        for h in range(N_HEADS):
            pv = None
            for i in range(s.shape[1] // PAGE_SIZE):
                part = jnp.dot(pb[h * hr:(h + 1) * hr, i * PAGE_SIZE:(i + 1) * PAGE_SIZE], value_of(h, i),
                               preferred_element_type=F32)
                pv = part if pv is None else pv + part
            pvs.append(pv)
        acc_ref[...] = acc_ref[...] * corr + jnp.concatenate(pvs, axis=0)
        m_ref[...] = m_new

    s = jnp.concatenate(
        [jnp.concatenate(
            [jnp.dot(qbd_ref[h], kr[0, 0, h * hw:(h + 1) * hw, :].astype(BF16), preferred_element_type=F32)
             for kr in k_refs], axis=1) for h in range(N_HEADS)], axis=0)
    update(s, lambda h, i: v_refs[i][0, 0, pl.ds(h, PAGE_SIZE, stride=N_HEADS), :].astype(BF16))

    @pl.when(p == pl.num_programs(1) - 1)
    def _():
        s_new = jnp.concatenate(
            [lax.dot_general(qbd_ref[h], kn_ref[0, :, h * hw:(h + 1) * hw].astype(BF16), NT_DIMS,
                             preferred_element_type=F32) for h in range(N_HEADS)], axis=0)
        tok = lax.broadcasted_iota(jnp.int32, s_new.shape, 0) % n_new
        key = lax.broadcasted_iota(jnp.int32, s_new.shape, 1)
        s_new = jnp.where((key < n_new) & (key <= tok), s_new, -jnp.inf)
        update(s_new, lambda h, i: vn_ref[0, :, h * hw:(h + 1) * hw].astype(BF16))
        o = acc_ref[...] / l_ref[...]
        lam = _lambda_value(lq1_ref, lk1_ref, lq2_ref, lk2_ref, lam_init)
        for h in range(N_HEADS):
            o0 = o[h * hr:h * hr + n_new]
            o1 = o[h * hr + n_new:(h + 1) * hr]
            o_ref[0, :, h * hw:(h + 1) * hw] = _diff_norm(o0, o1, lam, sg_ref[...], lam_init).astype(BF16)


def _paged(page_table, q, k_new, v_new, cache_kt, cache_v, lq1, lk1, lq2, lk2, subln_g, *,
           layer, pages_per_step, lam_init):
    n_batch, n_pages = page_table.shape
    n_new = q.shape[1]
    steps = n_pages // pages_per_step
    lam_spec = pl.BlockSpec((1, HEAD_DIM), lambda n, p, pt: (0, 0))

    def kpage_spec(i):
        return pl.BlockSpec((1, 1, D_MODEL, PAGE_SIZE),
                            lambda n, p, pt: (layer, pt[n * n_pages + p * pages_per_step + i], 0, 0))

    def vpage_spec(i):
        return pl.BlockSpec((1, 1, PAGE_SIZE * N_HEADS, 2 * HEAD_DIM),
                            lambda n, p, pt: (layer, pt[n * n_pages + p * pages_per_step + i], 0, 0))

    per_batch = lambda n, p, pt: (n, 0, 0)
    grid_spec = pltpu.PrefetchScalarGridSpec(
        num_scalar_prefetch=1,
        grid=(n_batch, steps),
        in_specs=[pl.BlockSpec((1, n_new, D_MODEL), per_batch),
                  pl.BlockSpec((1, PAGE_SIZE, D_MODEL), per_batch),
                  pl.BlockSpec((1, PAGE_SIZE, D_MODEL), per_batch),
                  lam_spec, lam_spec, lam_spec, lam_spec,
                  pl.BlockSpec((1, 2 * HEAD_DIM), lambda n, p, pt: (0, 0))]
                 + [kpage_spec(i) for i in range(pages_per_step)]
                 + [vpage_spec(i) for i in range(pages_per_step)],
        out_specs=pl.BlockSpec((1, n_new, D_MODEL), per_batch),
        scratch_shapes=[pltpu.VMEM((N_HEADS, 2 * n_new, 2 * HEAD_DIM), BF16),
                        pltpu.VMEM((N_SUB * n_new, 1), F32),
                        pltpu.VMEM((N_SUB * n_new, 1), F32),
                        pltpu.VMEM((N_SUB * n_new, 2 * HEAD_DIM), F32)],
    )
    return pl.pallas_call(
        functools.partial(_paged_kernel, pages_per_step=pages_per_step, n_new=n_new, lam_init=lam_init),
        grid_spec=grid_spec,
        out_shape=jax.ShapeDtypeStruct((n_batch, n_new, D_MODEL), BF16),
        compiler_params=_params(2),
        name="sample_attn",
    )(page_table.reshape(-1), q, k_new, v_new, lq1, lk1, lq2, lk2, subln_g,
      *([cache_kt] * pages_per_step), *([cache_v] * pages_per_step))


def _attn_out_kernel(a_ref, x_ref, mod_ref, wo_ref, g_ref, b_ref, wr_ref, x1_ref, h1_ref, lg_ref):
    y = jnp.dot(a_ref[...], wo_ref[...], preferred_element_type=F32)
    _mixer_epilogue(x_ref[...], y, mod_ref, g_ref, b_ref, wr_ref, x1_ref, h1_ref, lg_ref)


def _attn_out(a, x, mods, w_o, ln_g, ln_b, w_router, *, tm, tiles_per_batch, n_all, first_tile, merged=()):
    t_tok = x.shape[0]
    rows = mods.shape[2]
    tok = lambda i: (i, 0)
    moe_specs, moe_shapes = _moe_input_outputs(n_all, tm, first_tile)
    n_in = 7
    return pl.pallas_call(
        _skip_merged_refs(_attn_out_kernel, n_in, merged),
        grid=(t_tok // tm,),
        in_specs=[pl.BlockSpec((tm, D_MODEL), tok), pl.BlockSpec((tm, D_MODEL), tok),
                  _mod_spec(rows, tiles_per_batch), _const_spec(w_o.shape),
                  _const_spec((1, D_MODEL)), _const_spec((1, D_MODEL)),
                  _const_spec((D_MODEL, N_EXPERTS))] + [pl.BlockSpec(memory_space=pl.ANY)] * len(merged),
        out_specs=[pl.BlockSpec((tm, D_MODEL), tok)] + moe_specs,
        out_shape=[jax.ShapeDtypeStruct((t_tok, D_MODEL), F32)] + moe_shapes,
        input_output_aliases={n_in + k: 1 + k for k in range(len(merged))},
        compiler_params=_params(1),
        name="attn_out",
    )(a, x, mods, w_o, ln_g, ln_b, w_router, *merged)


def _rope_tables(pos):
    half = HEAD_DIM // 2
    inv_freq = ROPE_THETA ** (-jnp.arange(half, dtype=F32) * 2.0 / HEAD_DIM)
    ang = pos[:, None] * inv_freq[None, :]
    cos, sin = jnp.cos(ang), jnp.sin(ang)
    reps = LANES // HEAD_DIM
    cos_t = jnp.tile(jnp.concatenate([cos, cos], axis=1), (1, reps))
    sin_t = jnp.tile(jnp.concatenate([-sin, sin], axis=1), (1, reps))
    return cos_t, sin_t, cos.T, sin.T


def kernel(x_prompt, x_sample, state_pool, cache_k, cache_v, page_table, c_prompt, c_sample, w_ada, b_ada, ln1_g, ln1_b, ln2_g, ln2_b, w_pool, pool_scale, w_qkv, lambda_q1, lambda_k1, lambda_q2, lambda_k2, subln_g, w_o, w_router, router_bias, w_gate, w_up, w_down, ws_gate, ws_up, ws_down):
    n_p, l_p, d = x_prompt.shape
    n_s, l_s, _ = x_sample.shape
    t_p, t_s = n_p * l_p, n_s * l_s
    tm_p = 512
    tpb_p = l_p // tm_p
    tm_route = 256
    tb_moe, rm_moe, sub_moe = _moe_tiling(t_p + t_s)
    assert t_p % t_s == 0 and (t_p + t_s) % tm_route == 0
    row = lambda a: a.reshape(1, -1)
    experts = tuple(w.astype(BF16) for w in (w_gate, w_up, w_down, ws_gate, ws_up, ws_down))

    mods = _ada(jnp.concatenate([c_prompt, c_sample], axis=0), w_ada, b_ada)
    xp = x_prompt.reshape(t_p, d)
    xs = x_sample.reshape(t_s, d)
    cache_kt = jnp.transpose(cache_k, (0, 1, 3, 4, 5, 2)).reshape(cache_k.shape[0], cache_k.shape[1], d,
                                                                   PAGE_SIZE)
    cache_vr = cache_v.reshape(cache_v.shape[0], cache_v.shape[1], PAGE_SIZE * N_HEADS, 2 * HEAD_DIM)
    cos_p, sin_p, cos_pt, sin_pt = _rope_tables(jnp.arange(l_p, dtype=F32))
    cos_s, sin_s, _, _ = _rope_tables(jnp.arange(l_s, dtype=F32) + PAST_LEN)
    cos_s, sin_s = jnp.tile(cos_s, (n_s, 1)), jnp.tile(sin_s, (n_s, 1))

    pool_p, pool_s, kp_l, vp_l, ks_l, vs_l = [], [], [], [], [], []
    for i in range(DEPTH):
        j = i // 2
        mods_p = mods[i, :, :n_p].reshape(6, n_p, 1, d)
        mods_s_batch = mods[i, :, n_p:].reshape(6, n_s, 1, d)
        mods_s_tok = jnp.repeat(mods[i, :, n_p:], l_s, axis=1).reshape(6, 1, t_s, d)
        g1, b1, g2, b2 = row(ln1_g[i]), row(ln1_b[i]), row(ln2_g[i]), row(ln2_b[i])
        if i % 2 == 0:
            wp = w_pool[j].astype(BF16)
            ps = row(pool_scale[j])
            xp1, h1, lg, st_p = _pool_layer(xp, xp, mods_p, wp, ps, g1, b1, w_router[i], tm=tm_p,
                                            tiles_per_batch=tpb_p, start=0, halo_is_state=False,
                                            n_all=t_p + t_s, first_tile=0)
            hist = jnp.pad(state_pool[j], ((0, 0), (POOL_HALO - state_pool.shape[2], 0), (0, 0)))
            xs1, h1, lg, st_s = _pool_layer(xs, hist, mods_s_batch, wp, ps, g1, b1, w_router[i], tm=l_s,
                                            tiles_per_batch=1, start=PAST_LEN, halo_is_state=True,
                                            n_all=t_p + t_s, first_tile=t_p // l_s, merged=(h1, lg))
            keep = state_pool.shape[2]
            pool_p.append(st_p[:, POOL_HALO - keep:])
            pool_s.append(st_s[:, POOL_HALO - keep:])
        else:
            lam_init = 0.8 - 0.6 * math.exp(-0.3 * i)
            wq = w_qkv[j].astype(BF16)
            wo = w_o[j].astype(BF16)
            lams = (row(lambda_q1[j]), row(lambda_k1[j]), row(lambda_q2[j]), row(lambda_k2[j]))
            sg = row(subln_g[j])
            wkt = w_qkv[j][:, d:2 * d].T.astype(BF16)
            qp, kpt, vp = _qkv_prompt(xp, mods_p, wq[:, :d], wkt, wq[:, 2 * d:], cos_p, sin_p, cos_pt, sin_pt,
                                      n_batch=n_p, seq=l_p, tm=tm_p)
            ap = _flash(qp, kpt, vp, *lams, sg, n_batch=n_p, seq=l_p, tq=512, tk=512, lam_init=lam_init)
            xp1, h1, lg = _attn_out(ap, xp, mods_p, wo, g1, b1, w_router[i], tm=tm_p, tiles_per_batch=tpb_p,
                                    n_all=t_p + t_s, first_tile=0)
            qs, ks, vs = _qkv(xs, mods_s_tok, wq, cos_s, sin_s, tm=t_s, tiles_per_batch=1, table_tiles=1)
            pad_new = lambda a: jnp.pad(a.reshape(n_s, l_s, d), ((0, 0), (0, PAGE_SIZE - l_s), (0, 0)))
            a_s = _paged(page_table, qs.reshape(n_s, l_s, d), pad_new(ks), pad_new(vs), cache_kt, cache_vr,
                         *lams, sg, layer=j, pages_per_step=16, lam_init=lam_init)
            xs1, h1, lg = _attn_out(a_s.reshape(t_s, d), xs, mods_s_tok, wo, g1, b1, w_router[i], tm=t_s,
                                    tiles_per_batch=1, n_all=t_p + t_s, first_tile=t_p // t_s, merged=(h1, lg))
            kp_l.append(kpt.reshape(n_p, N_HEADS, 2, HEAD_DIM, l_p).transpose(0, 4, 1, 2, 3))
            vp_l.append(vp.reshape(n_p, l_p, N_HEADS, 2 * HEAD_DIM))
            ks_l.append(ks.reshape(n_s, l_s, N_HEADS, 2, HEAD_DIM))
            vs_l.append(vs.reshape(n_s, l_s, N_HEADS, 2 * HEAD_DIM))
        picks = _router(lg, router_bias[i].reshape(N_EXPERTS, 1), tm=tm_route)
        lists = _dispatch_lists(*picks, tb=tb_moe)
        y = _moe_routed(*lists, h1, *experts, layer=i, tb=tb_moe, rm=rm_moe, sub=sub_moe)
        xp = _moe_out(xp1, y, mods_p, g2, b2, tm=tm_p, tiles_per_batch=tpb_p, y_first_tile=0)
        xs = _moe_out(xs1, y, mods_s_tok, g2, b2, tm=t_s, tiles_per_batch=1, y_first_tile=t_p // t_s)
    return (xp.reshape(n_p, l_p, d), xs.reshape(n_s, l_s, d), jnp.stack(pool_p), jnp.stack(pool_s),
            jnp.stack(kp_l), jnp.stack(vp_l), jnp.stack(ks_l), jnp.stack(vs_l))
```
